```python
import math
import jax, jax.numpy as jnp
from jax import lax
import numpy as np

D_MODEL = 2048
BATCH = 1
SEQ = 16384
DEPTH = 2

SB_HEADS = 8
SB_HEAD_DIM = 128
SB_WIDTH = SB_HEADS * SB_HEAD_DIM
Q_BLOCK = 128
S5_WIDTH = 1024
S5_GROUP = 16
S5_GROUPS = S5_WIDTH // S5_GROUP
S5_STATE = 64
ML_HEADS = 4
ML_HEAD_DIM = 256
ML_WIDTH = ML_HEADS * ML_HEAD_DIM
ML_CHUNK = 64
ML_CONV = 4
N_GROUPS = 4
EXPERTS_PER_GROUP = 8
N_EXPERTS = N_GROUPS * EXPERTS_PER_GROUP
TOP_K = 2
EXPERT_FF = 256
EPS = 1e-6

IN_SIZES = (SB_WIDTH, SB_WIDTH, SB_WIDTH,
            S5_WIDTH,
            ML_WIDTH, ML_WIDTH, ML_WIDTH, ML_WIDTH,
            ML_HEADS, ML_HEADS,
            D_MODEL, D_MODEL, D_MODEL)
IN_COLS = sum(IN_SIZES)

kernel_name = "hybrid_sb_s5_mlstm_hmoe_adaln"


def rms_norm(x, g):
    xf = x.astype(jnp.float32)
    y = xf * lax.rsqrt(jnp.mean(xf * xf, axis=-1, keepdims=True) + EPS)
    return (y * g.astype(jnp.float32)).astype(x.dtype)


def modulate(h, shift, scale):
    return h * (1.0 + scale[:, None, :]) + shift[:, None, :]


def stick_breaking_attention(q, k, v):
    bsz, seq, nh, dh = q.shape
    nb = seq // Q_BLOCK
    qb = q.reshape(bsz, nb, Q_BLOCK, nh, dh).transpose(1, 0, 3, 2, 4)
    kpos = jnp.arange(seq)
    scale = dh ** -0.5

    def block(args):
        qi, bi = args
        qpos = bi * Q_BLOCK + jnp.arange(Q_BLOCK)
        z = jnp.einsum('bhqd,bshd->bhqs', qi, k).astype(jnp.float32) * scale
        causal = kpos[None, :] < qpos[:, None]
        log_beta = jax.nn.log_sigmoid(z)
        log_keep = jnp.where(causal, jax.nn.log_sigmoid(-z), 0.0)
        rest = lax.cumsum(log_keep, axis=3, reverse=True) - log_keep
        w = jnp.where(causal, jnp.exp(log_beta + rest), 0.0)
        return jnp.einsum('bhqs,bshd->bhqd', w.astype(v.dtype), v)

    out = lax.map(block, (qb, jnp.arange(nb)))
    return out.transpose(1, 0, 3, 2, 4).reshape(bsz, seq, nh * dh)


def _linear_recurrence_combine(left, right):
    a_l, b_l = left
    a_r, b_r = right
    return a_r * a_l, a_r * b_l + b_r


def s5_layer(u, lam_re, lam_im, log_dt, b_re, b_im, c_re, c_im, d_skip, w_glu):
    bsz, seq, _ = u.shape
    f32 = jnp.float32
    uf = u.astype(f32).reshape(bsz, seq, S5_GROUPS, S5_GROUP)
    lam = lax.complex(lam_re.astype(f32), lam_im.astype(f32))
    dt = jnp.exp(log_dt.astype(f32))[:, None]
    lam_bar = jnp.exp(lam * dt)
    b_bar = ((lam_bar - 1.0) / lam)[:, :, None] * lax.complex(b_re.astype(f32), b_im.astype(f32))
    bu = jnp.einsum('bsgh,gph->bsgp', uf.astype(jnp.complex64), b_bar)
    a = jnp.broadcast_to(lam_bar, bu.shape)
    _, states = lax.associative_scan(_linear_recurrence_combine, (a, bu), axis=1)
    c_mat = lax.complex(c_re.astype(f32), c_im.astype(f32))
    y = jnp.einsum('bsgp,ghp->bsgh', states, c_mat).real \
        + d_skip.astype(f32).reshape(S5_GROUPS, S5_GROUP) * uf
    y = jax.nn.gelu(y.reshape(bsz, seq, S5_WIDTH))
    y = y * jax.nn.sigmoid(y @ w_glu.astype(f32))
    return y.astype(u.dtype)


def causal_depthwise_conv(x, w):
    width, ch = w.shape
    return lax.conv_general_dilated(x, w[:, None, :].astype(x.dtype), window_strides=(1,),
                                    padding=[(width - 1, 0)],
                                    dimension_numbers=('NWC', 'WIO', 'NWC'),
                                    feature_group_count=ch)


def mlstm_chunkwise(q, k, v, i_pre, f_pre):
    bsz, seq, nh, dh = q.shape
    f32 = jnp.float32
    L = ML_CHUNK
    nc = seq // L
    q = q.astype(f32).reshape(bsz, nc, L, nh, dh)
    k = k.astype(f32).reshape(bsz, nc, L, nh, dh) * (dh ** -0.5)
    v = v.astype(f32).reshape(bsz, nc, L, nh, dh)
    log_i = i_pre.astype(f32).reshape(bsz, nc, L, nh)
    log_f = jax.nn.log_sigmoid(f_pre.astype(f32)).reshape(bsz, nc, L, nh)
    b = jnp.cumsum(log_f, axis=2)
    b_tot = b[:, :, -1]
    a = b_tot[:, :, None] - b + log_i
    m_loc = jnp.max(a, axis=2)
    wa = jnp.exp(a - m_loc[:, :, None])
    d_c = jnp.einsum('bclh,bclhv,bclhk->bchvk', wa, v, k)
    d_n = jnp.einsum('bclh,bclhk->bchk', wa, k)

    def step(carry, inp):
        c_st, n_st, m_st = carry
        dc, dn, ml, bt = inp
        m_new = jnp.maximum(bt + m_st, ml)
        s_old = jnp.exp(bt + m_st - m_new)
        s_new = jnp.exp(ml - m_new)
        c_new = s_old[..., None, None] * c_st + s_new[..., None, None] * dc
        n_new = s_old[..., None] * n_st + s_new[..., None] * dn
        return (c_new, n_new, m_new), (c_st, n_st, m_st)

    init = (jnp.zeros((bsz, nh, dh, dh), f32), jnp.zeros((bsz, nh, dh), f32), jnp.zeros((bsz, nh), f32))
    _, (c0, n0, m0) = lax.scan(step, init, (jnp.moveaxis(d_c, 1, 0), jnp.moveaxis(d_n, 1, 0),
                                             jnp.moveaxis(m_loc, 1, 0), jnp.moveaxis(b_tot, 1, 0)))
    c0 = jnp.moveaxis(c0, 0, 1)
    n0 = jnp.moveaxis(n0, 0, 1)
    m0 = jnp.moveaxis(m0, 0, 1)
    tril = jnp.tril(jnp.ones((L, L), dtype=bool))
    log_d = b[:, :, :, None, :] - b[:, :, None, :, :] + log_i[:, :, None, :, :]
    log_d = jnp.where(tril[None, None, :, :, None], log_d, -jnp.inf)
    inter_log = b + m0[:, :, None, :]
    m_t = jnp.maximum(inter_log, jnp.max(log_d, axis=3))
    d_w = jnp.exp(log_d - m_t[:, :, :, None, :])
    w_inter = jnp.exp(inter_log - m_t)
    s_mat = d_w * jnp.einsum('bclhd,bcshd->bclsh', q, k)
    num = jnp.einsum('bclsh,bcshv->bclhv', s_mat, v) \
        + w_inter[..., None] * jnp.einsum('bchvk,bclhk->bclhv', c0, q)
    den = jnp.sum(s_mat, axis=3) + w_inter * jnp.einsum('bchk,bclhk->bclh', n0, q)
    h = num / jnp.maximum(jnp.abs(den), jnp.exp(-m_t))[..., None]
    return h.reshape(bsz, seq, nh * dh)


def hierarchical_moe(h, w_group, b_group, w_router, b_router, w1, w3, w2):
    bsz, seq, dm = h.shape
    t = h.reshape(-1, dm)
    g_logits = (t @ w_group).astype(jnp.float32) + b_group.astype(jnp.float32)
    g_prob = jax.nn.softmax(g_logits, axis=-1)
    g_idx = jnp.argmax(g_logits, axis=-1)
    g_w = jnp.take_along_axis(g_prob, g_idx[:, None], axis=-1)
    e_logits = ((t @ w_router).astype(jnp.float32) + b_router.astype(jnp.float32)).reshape(-1, N_GROUPS, EXPERTS_PER_GROUP)
    e_in_group = jnp.take_along_axis(e_logits, g_idx[:, None, None], axis=1)[:, 0]
    top_v, top_i = lax.top_k(e_in_group, TOP_K)
    top_w = jax.nn.softmax(top_v, axis=-1) * g_w
    within = jnp.sum(jax.nn.one_hot(top_i, EXPERTS_PER_GROUP, dtype=jnp.float32) * top_w[..., None], axis=1)
    gates = (jax.nn.one_hot(g_idx, N_GROUPS, dtype=jnp.float32)[:, :, None] * within[:, None, :]).reshape(-1, N_EXPERTS)
    hid = jax.nn.silu(jnp.einsum('td,edf->tef', t, w1)) * jnp.einsum('td,edf->tef', t, w3)
    y = jnp.einsum('tef,efd->td', hid * gates.astype(hid.dtype)[:, :, None], w2)
    return y.reshape(bsz, seq, dm)


def setup_inputs(seed: int = 0) -> dict:
    key = jax.random.key(seed)
    ks = jax.random.split(key, 40)
    f32 = jnp.float32
    L = DEPTH
    D = D_MODEL

    def nrm(k, shape, scale):
        return jax.random.normal(k, shape, f32) * scale

    n_idx = jnp.arange(S5_STATE, dtype=f32)
    return {
        'x': nrm(ks[0], (BATCH, SEQ, D), 1.0),
        'c': nrm(ks[1], (BATCH, D), 1.0),
        'norm_mix_g': 1.0 + nrm(ks[2], (L, D), 0.02),
        'norm_moe_g': 1.0 + nrm(ks[3], (L, D), 0.02),
        'final_g': 1.0 + nrm(ks[4], (D,), 0.02),
        'w_ada': nrm(ks[5], (L, D, 6 * D), 0.5 * D ** -0.5),
        'b_ada': nrm(ks[6], (L, 6 * D), 0.02),
        'w_in': nrm(ks[7], (L, D, IN_COLS), D ** -0.5),
        'ml_conv': nrm(ks[8], (L, ML_CONV, 2 * ML_WIDTH), 0.5),
        'ml_i_bias': nrm(ks[9], (L, ML_HEADS), 0.1),
        'ml_f_bias': jnp.linspace(3.0, 6.0, ML_HEADS, dtype=f32)[None, :] + nrm(ks[10], (L, ML_HEADS), 0.1),
        's5_lam_re': -0.5 + nrm(ks[11], (L, S5_GROUPS, S5_STATE), 0.01),
        's5_lam_im': math.pi * n_idx[None, None, :] + nrm(ks[12], (L, S5_GROUPS, S5_STATE), 0.01),
        's5_log_dt': jax.random.uniform(ks[13], (L, S5_GROUPS), f32, math.log(1e-3), math.log(1e-1)),
        's5_b_re': nrm(ks[14], (L, S5_GROUPS, S5_STATE, S5_GROUP), (2 * S5_GROUP) ** -0.5),
        's5_b_im': nrm(ks[15], (L, S5_GROUPS, S5_STATE, S5_GROUP), (2 * S5_GROUP) ** -0.5),
        's5_c_re': nrm(ks[16], (L, S5_GROUPS, S5_GROUP, S5_STATE), S5_STATE ** -0.5),
        's5_c_im': nrm(ks[17], (L, S5_GROUPS, S5_GROUP, S5_STATE), S5_STATE ** -0.5),
        's5_d': nrm(ks[18], (L, S5_WIDTH), 0.5),
        's5_w_glu': nrm(ks[19], (L, S5_WIDTH, S5_WIDTH), S5_WIDTH ** -0.5),
        'p_sb': nrm(ks[20], (L, SB_WIDTH, D), SB_WIDTH ** -0.5),
        'p_s5': nrm(ks[21], (L, S5_WIDTH, D), S5_WIDTH ** -0.5),
        'p_ml': nrm(ks[22], (L, ML_WIDTH, D), ML_WIDTH ** -0.5),
        'w_out': nrm(ks[23], (L, D, D), D ** -0.5),
        'moe_w_group': nrm(ks[24], (L, D, N_GROUPS), D ** -0.5),
        'moe_b_group': nrm(ks[25], (L, N_GROUPS), 0.01),
        'moe_w_router': nrm(ks[26], (L, D, N_EXPERTS), D ** -0.5),
        'moe_b_router': nrm(ks[27], (L, N_EXPERTS), 0.01),
        'moe_w1': nrm(ks[28], (L, N_EXPERTS, D, EXPERT_FF), D ** -0.5),
        'moe_w3': nrm(ks[29], (L, N_EXPERTS, D, EXPERT_FF), D ** -0.5),
        'moe_w2': nrm(ks[30], (L, N_EXPERTS, EXPERT_FF, D), EXPERT_FF ** -0.5),
    }


def reference(x, c, norm_mix_g, norm_moe_g, final_g, w_ada, b_ada, w_in, ml_conv, ml_i_bias, ml_f_bias,
              s5_lam_re, s5_lam_im, s5_log_dt, s5_b_re, s5_b_im, s5_c_re, s5_c_im, s5_d, s5_w_glu,
              p_sb, p_s5, p_ml, w_out, moe_w_group, moe_b_group, moe_w_router, moe_b_router,
              moe_w1, moe_w3, moe_w2):
    bsz, seq, _ = x.shape
    split_points = [int(p) for p in np.cumsum(IN_SIZES)[:-1]]
    c_act = jax.nn.silu(c)
    for l in range(DEPTH):
        mod = c_act @ w_ada[l] + b_ada[l]
        shift1, scale1, gate1, shift2, scale2, gate2 = jnp.split(mod, 6, axis=-1)

        h = modulate(rms_norm(x, norm_mix_g[l]), shift1, scale1)
        proj = h @ w_in[l]
        (sb_q, sb_k, sb_v, s5_u, ml_q, ml_k, ml_v, ml_o, ml_i, ml_f,
         g_sb, g_s5, g_ml) = jnp.split(proj, split_points, axis=-1)

        sb_shape = (bsz, seq, SB_HEADS, SB_HEAD_DIM)
        y_sb = stick_breaking_attention(sb_q.reshape(sb_shape), sb_k.reshape(sb_shape), sb_v.reshape(sb_shape))

        y_s5 = s5_layer(s5_u, s5_lam_re[l], s5_lam_im[l], s5_log_dt[l], s5_b_re[l], s5_b_im[l],
                        s5_c_re[l], s5_c_im[l], s5_d[l], s5_w_glu[l])

        qk = jax.nn.silu(causal_depthwise_conv(jnp.concatenate([ml_q, ml_k], axis=-1), ml_conv[l]))
        ml_q_c, ml_k_c = jnp.split(qk, 2, axis=-1)
        ml_shape = (bsz, seq, ML_HEADS, ML_HEAD_DIM)
        y_ml = mlstm_chunkwise(ml_q_c.reshape(ml_shape), ml_k_c.reshape(ml_shape), ml_v.reshape(ml_shape),
                               ml_i + ml_i_bias[l], ml_f + ml_f_bias[l])
        y_ml = (y_ml * jax.nn.sigmoid(ml_o.astype(jnp.float32))).astype(x.dtype)

        merged = jax.nn.sigmoid(g_sb) * (y_sb @ p_sb[l]) \
            + jax.nn.sigmoid(g_s5) * (y_s5 @ p_s5[l]) \
            + jax.nn.sigmoid(g_ml) * (y_ml @ p_ml[l])
        x = x + gate1[:, None, :] * (merged @ w_out[l])

        h = modulate(rms_norm(x, norm_moe_g[l]), shift2, scale2)
        x = x + gate2[:, None, :] * hierarchical_moe(h, moe_w_group[l], moe_b_group[l], moe_w_router[l],
                                                     moe_b_router[l], moe_w1[l], moe_w3[l], moe_w2[l])
    return rms_norm(x, final_g)
```

```python
import functools
import math

import jax
import jax.numpy as jnp
from jax import lax
from jax.experimental import pallas as pl
from jax.experimental.pallas import tpu as pltpu

F32 = jnp.float32
BF16 = jnp.bfloat16
HIGHEST = lax.Precision.HIGHEST

D_MODEL = 2048
DEPTH = 2
SB_HEADS = 8
SB_HEAD_DIM = 128
SB_WIDTH = SB_HEADS * SB_HEAD_DIM
S5_WIDTH = 1024
S5_GROUP = 16
S5_GROUPS = S5_WIDTH // S5_GROUP
S5_STATE = 64
ML_HEADS = 4
ML_HEAD_DIM = 256
ML_WIDTH = ML_HEADS * ML_HEAD_DIM
ML_CONV = 4
N_GROUPS = 4
EXPERTS_PER_GROUP = 8
N_EXPERTS = N_GROUPS * EXPERTS_PER_GROUP
EXPERT_FF = 256
EPS = 1e-6

LANES = 128
SUBLANES = 8
VMEM_LIMIT = 52 * 1024 * 1024

COL_SB_Q, COL_SB_K, COL_SB_V = 0, 1024, 2048
COL_S5_U = 3072
COL_ML_Q, COL_ML_K, COL_ML_V, COL_ML_O = 4096, 5120, 6144, 7168
COL_GATES = 8192
N_GATE_COLS = 2 * ML_HEADS
COL_G_SB, COL_G_S5, COL_G_ML = 8192, 10240, 12288
MAIN_COLS = 14336

S5_CHUNK = 64
ML_CHUNK = 128
SB_BLOCK = 128
SB_LOG_CUTOFF = -105.0


def _params(sem, vmem=VMEM_LIMIT):
    return pltpu.CompilerParams(dimension_semantics=sem, vmem_limit_bytes=vmem)


def _log_sigmoid(z):
    return jnp.minimum(z, 0.0) - jnp.log1p(jnp.exp(-jnp.abs(z)))


def _ada_kernel(c_ref, w_ref, b_ref, o_ref):
    c = c_ref[...]
    ca = c * jax.nn.sigmoid(c)
    r = jnp.dot(ca, w_ref[...], precision=HIGHEST, preferred_element_type=F32)
    o_ref[...] = r[0:1, :] + b_ref[...]


def ada_mod(c, w_ada, b_ada):
    depth, d, n = w_ada.shape
    tn = 1024
    c8 = jnp.broadcast_to(c.reshape(1, d), (SUBLANES, d))
    return pl.pallas_call(
        _ada_kernel,
        out_shape=jax.ShapeDtypeStruct((depth, 1, n), F32),
        grid=(depth, n // tn),
        in_specs=[pl.BlockSpec((SUBLANES, d), lambda l, j: (0, 0)),
                  pl.BlockSpec((None, d, tn), lambda l, j: (l, 0, j)),
                  pl.BlockSpec((None, 1, tn), lambda l, j: (l, 0, j))],
        out_specs=pl.BlockSpec((None, 1, tn), lambda l, j: (l, 0, j)),
        compiler_params=_params(("parallel", "parallel")),
        name="ada_mod",
    )(c8, w_ada, b_ada.reshape(depth, 1, n))


def _norm_mod(x, g, shift, scale):
    ms = jnp.mean(x * x, axis=-1, keepdims=True)
    y = x * lax.rsqrt(ms + EPS) * g
    return y * (1.0 + scale) + shift


def _inproj_kernel(x_ref, g_ref, sh_ref, sc_ref, w_ref, wg_ref, o_ref, og_ref, h_scr):
    @pl.when(pl.program_id(1) == 0)
    def _():
        hb = _norm_mod(x_ref[...], g_ref[...], sh_ref[...], sc_ref[...]).astype(BF16)
        h_scr[...] = hb
        og_ref[...] = jnp.dot(hb, wg_ref[...], preferred_element_type=F32)

    o_ref[...] = jnp.dot(h_scr[...], w_ref[...], preferred_element_type=F32).astype(o_ref.dtype)


def in_proj(x, g, shift, scale, w_main, w_gate):
    t, d = x.shape
    n = w_main.shape[1]
    tm = min(1024, t)
    tn = 512
    vec = pl.BlockSpec((1, d), lambda i, j: (0, 0))
    return pl.pallas_call(
        _inproj_kernel,
        out_shape=(jax.ShapeDtypeStruct((t, n), BF16), jax.ShapeDtypeStruct((t, LANES), F32)),
        grid=(t // tm, n // tn),
        in_specs=[pl.BlockSpec((tm, d), lambda i, j: (i, 0)), vec, vec, vec,
                  pl.BlockSpec((d, tn), lambda i, j: (0, j)),
                  pl.BlockSpec((d, LANES), lambda i, j: (0, 0))],
        out_specs=(pl.BlockSpec((tm, tn), lambda i, j: (i, j)),
                   pl.BlockSpec((tm, LANES), lambda i, j: (i, 0))),
        scratch_shapes=[pltpu.VMEM((tm, d), BF16)],
        compiler_params=_params(("parallel", "arbitrary")),
        name="in_proj",
    )(x, g, shift, scale, w_main, w_gate)


def _sb_kernel(q_ref, k_ref, v_ref, o_ref):
    blk = SB_BLOCK
    i = pl.program_id(1)
    q = q_ref[...]
    scale = SB_HEAD_DIM ** -0.5
    row = lax.broadcasted_iota(jnp.int32, (blk, blk), 0)
    col = lax.broadcasted_iota(jnp.int32, (blk, blk), 1)
    after = jnp.where(row > col, 1.0, 0.0).astype(BF16)
    causal = col < row

    def step(kb, run, acc, diagonal):
        start = pl.multiple_of(kb * blk, blk)
        kblk = k_ref[pl.ds(start, blk), :]
        vblk = v_ref[pl.ds(start, blk), :]
        z = lax.dot_general(q, kblk, (((1,), (1,)), ((), ())), preferred_element_type=F32) * scale
        softplus = jnp.maximum(z, 0.0) + jnp.log1p(jnp.exp(-jnp.abs(z)))
        log_keep = -softplus
        log_beta = z - softplus
        if diagonal:
            log_keep = jnp.where(causal, log_keep, 0.0)
        hi = log_keep.astype(BF16)
        lo = (log_keep - hi.astype(F32)).astype(BF16)
        later = (jnp.dot(hi, after, preferred_element_type=F32)
                 + jnp.dot(lo, after, preferred_element_type=F32))
        w = jnp.exp(log_beta + later + run)
        if diagonal:
            w = jnp.where(causal, w, 0.0)
        acc = acc + jnp.dot(w.astype(BF16), vblk, preferred_element_type=F32)
        run = run + jnp.sum(log_keep, axis=1, keepdims=True)
        return run, acc

    run0 = jnp.zeros((blk, 1), F32)
    acc0 = jnp.zeros((blk, SB_HEAD_DIM), F32)
    run, acc = step(i, run0, acc0, True)

    def cond(carry):
        kb, run, _ = carry
        return jnp.logical_and(kb >= 0, jnp.max(run) > SB_LOG_CUTOFF)

    def body(carry):
        kb, run, acc = carry
        run, acc = step(kb, run, acc, False)
        return kb - 1, run, acc

    _, _, acc = lax.while_loop(cond, body, (i - 1, run, acc))
    o_ref[...] = acc.astype(o_ref.dtype)


def sb_attention(proj):
    t = proj.shape[0]
    blk = SB_BLOCK
    qb, kb, vb = COL_SB_Q // blk, COL_SB_K // blk, COL_SB_V // blk
    return pl.pallas_call(
        _sb_kernel,
        out_shape=jax.ShapeDtypeStruct((t, SB_WIDTH), BF16),
        grid=(SB_HEADS, t // blk),
        in_specs=[pl.BlockSpec((blk, SB_HEAD_DIM), lambda h, i: (i, qb + h)),
                  pl.BlockSpec((t, SB_HEAD_DIM), lambda h, i: (0, kb + h)),
                  pl.BlockSpec((t, SB_HEAD_DIM), lambda h, i: (0, vb + h))],
        out_specs=pl.BlockSpec((blk, SB_HEAD_DIM), lambda h, i: (i, h)),
        compiler_params=_params(("parallel", "arbitrary")),
        name="sb_attention",
    )(proj, proj, proj)


def s5_weights(lam_re, lam_im, log_dt, b_re, b_im, c_re, c_im, d_skip):
    lc, g, p, hh = S5_CHUNK, S5_GROUPS, S5_STATE, S5_GROUP
    dt = jnp.exp(log_dt)[:, None]
    zr, zi = lam_re * dt, lam_im * dt
    e1 = jnp.exp(zr)
    lbr, lbi = e1 * jnp.cos(zi), e1 * jnp.sin(zi)
    den = lam_re * lam_re + lam_im * lam_im
    fr = ((lbr - 1.0) * lam_re + lbi * lam_im) / den
    fi = (lbi * lam_re - (lbr - 1.0) * lam_im) / den
    bbr = fr[:, :, None] * b_re - fi[:, :, None] * b_im
    bbi = fr[:, :, None] * b_im + fi[:, :, None] * b_re
    tau = jnp.arange(lc + 1, dtype=F32)[None, :, None]
    mag = jnp.exp(zr[:, None, :] * tau)
    pwr = mag * jnp.cos(zi[:, None, :] * tau)
    pwi = mag * jnp.sin(zi[:, None, :] * tau)
    cpr = c_re[:, None] * pwr[:, :, None, :] - c_im[:, None] * pwi[:, :, None, :]
    cpi = c_re[:, None] * pwi[:, :, None, :] + c_im[:, None] * pwr[:, :, None, :]
    kern = (jnp.einsum('gthp,gpk->gthk', cpr[:, :lc], bbr, precision=HIGHEST)
            - jnp.einsum('gthp,gpk->gthk', cpi[:, :lc], bbi, precision=HIGHEST))
    s_idx = jnp.arange(lc)[:, None]
    t_idx = jnp.arange(lc)[None, :]
    lag = t_idx - s_idx
    kg = kern[:, jnp.clip(lag, 0, lc - 1)]
    kg = jnp.where((lag >= 0)[None, :, :, None, None], kg, 0.0)
    w_intra = kg.transpose(0, 1, 4, 2, 3).reshape(g, lc * hh, lc * hh).astype(BF16)
    rev_r, rev_i = pwr[:, lc - 1::-1][:, :lc], pwi[:, lc - 1::-1][:, :lc]
    wsr = rev_r[:, :, None, :] * bbr.transpose(0, 2, 1)[:, None] - rev_i[:, :, None, :] * bbi.transpose(0, 2, 1)[:, None]
    wsi = rev_r[:, :, None, :] * bbi.transpose(0, 2, 1)[:, None] + rev_i[:, :, None, :] * bbr.transpose(0, 2, 1)[:, None]
    w_state = jnp.concatenate([wsr, wsi], axis=-1).reshape(g, lc * hh, 2 * p).astype(BF16)
    wor = cpr[:, 1:].transpose(0, 3, 1, 2).reshape(g, p, lc * hh)
    woi = -cpi[:, 1:].transpose(0, 3, 1, 2).reshape(g, p, lc * hh)
    w_out = jnp.concatenate([wor, woi], axis=1).astype(BF16)
    a_chunk = jnp.stack([pwr[:, lc], pwi[:, lc]], axis=1)
    d_row = jnp.tile(d_skip.reshape(g, 1, hh), (1, lc, 1)).reshape(g, 1, lc * hh)
    return w_intra, w_state, w_out, a_chunk, d_row


def _s5_kernel(u_ref, wi_ref, ws_ref, wo_ref, a_ref, d_ref, o_ref, sr_scr, si_scr, xr_scr, xi_scr):
    p = S5_STATE
    nc = u_ref.shape[0]
    u = u_ref[...]
    s = jnp.dot(u, ws_ref[...], preferred_element_type=F32)
    sr_scr[...] = s[:, :p]
    si_scr[...] = s[:, p:]
    ar = a_ref[0:1, :]
    ai = a_ref[1:2, :]

    def body(c, carry):
        xr, xi = carry
        xr_scr[pl.ds(c, 1), :] = xr
        xi_scr[pl.ds(c, 1), :] = xi
        sr = sr_scr[pl.ds(c, 1), :]
        si = si_scr[pl.ds(c, 1), :]
        return ar * xr - ai * xi + sr, ar * xi + ai * xr + si

    zero = jnp.zeros((1, p), F32)
    lax.fori_loop(0, nc, body, (zero, zero))
    y = jnp.dot(u, wi_ref[...], preferred_element_type=F32)
    y = y + jnp.dot(xr_scr[...].astype(BF16), wo_ref[0:p, :], preferred_element_type=F32)
    y = y + jnp.dot(xi_scr[...].astype(BF16), wo_ref[p:2 * p, :], preferred_element_type=F32)
    y = y + d_ref[...] * u.astype(F32)
    o_ref[...] = jax.nn.gelu(y).astype(o_ref.dtype)


def s5_scan(u_g, w_intra, w_state, w_out, a_chunk, d_row):
    g, nc, width = u_g.shape
    p = S5_STATE
    grp = lambda n0, n1: pl.BlockSpec((None, n0, n1), lambda i: (i, 0, 0))
    return pl.pallas_call(
        _s5_kernel,
        out_shape=jax.ShapeDtypeStruct((g, nc, width), BF16),
        grid=(g,),
        in_specs=[grp(nc, width), grp(width, width), grp(width, 2 * p), grp(2 * p, width),
                  grp(2, p), grp(1, width)],
        out_specs=grp(nc, width),
        scratch_shapes=[pltpu.VMEM((nc, p), F32)] * 4,
        compiler_params=_params(("parallel",)),
        name="s5_scan",
    )(u_g, w_intra, w_state, w_out, a_chunk, d_row)


def _glu_kernel(y_ref, yc_ref, w_ref, o_ref):
    a = jnp.dot(y_ref[...], w_ref[...], preferred_element_type=F32)
    o_ref[...] = (yc_ref[...].astype(F32) * jax.nn.sigmoid(a)).astype(o_ref.dtype)


def s5_glu(y, w_glu):
    t, n = y.shape
    tm, tn = min(1024, t), 512
    return pl.pallas_call(
        _glu_kernel,
        out_shape=jax.ShapeDtypeStruct((t, n), BF16),
        grid=(t // tm, n // tn),
        in_specs=[pl.BlockSpec((tm, n), lambda i, j: (i, 0)),
                  pl.BlockSpec((tm, tn), lambda i, j: (i, j)),
                  pl.BlockSpec((n, tn), lambda i, j: (0, j))],
        out_specs=pl.BlockSpec((tm, tn), lambda i, j: (i, j)),
        compiler_params=_params(("parallel", "parallel")),
        name="s5_glu",
    )(y, y, w_glu)


ML_EXT = ML_HEAD_DIM + LANES


def _mlstm_kernel(q_ref, k_ref, v_ref, og_ref, g_ref, cw_ref, bias_ref, o_ref,
                  qbuf, kbuf, st_scr, m_scr):
    L = ML_CHUNK
    dh = ML_HEAD_DIM
    halo = SUBLANES
    c = pl.program_id(0)

    @pl.when(c == 0)
    def _():
        qbuf[0:halo, :] = jnp.zeros((halo, ML_WIDTH), F32)
        kbuf[0:halo, :] = jnp.zeros((halo, ML_WIDTH), F32)
        st_scr[...] = jnp.zeros(st_scr.shape, F32)
        m_scr[...] = jnp.zeros(m_scr.shape, F32)

    qbuf[halo:halo + L, :] = q_ref[...].astype(F32)
    kbuf[halo:halo + L, :] = k_ref[...].astype(F32)

    gp = g_ref[...] + bias_ref[...]
    lane = lax.broadcasted_iota(jnp.int32, gp.shape, 1)
    gl = jnp.where(lane < ML_HEADS, gp, _log_sigmoid(gp))
    r_i = lax.broadcasted_iota(jnp.int32, (L, L), 0)
    c_i = lax.broadcasted_iota(jnp.int32, (L, L), 1)
    tril = r_i >= c_i
    csum = jnp.dot(jnp.where(tril, 1.0, 0.0), gl, precision=HIGHEST,
                   preferred_element_type=F32)
    gl_t = gl.T
    csum_t = csum.T
    ones_col = (lax.broadcasted_iota(jnp.int32, (L, LANES), 1) == 0).astype(F32)

    for h in range(ML_HEADS):
        cs = slice(h * dh, (h + 1) * dh)

        def conv(buf, part):
            acc = None
            for j in range(ML_CONV):
                off = halo - (ML_CONV - 1) + j
                term = buf[off:off + L, cs] * cw_ref[j:j + 1, part * ML_WIDTH + h * dh:part * ML_WIDTH + (h + 1) * dh]
                acc = term if acc is None else acc + term
            return acc * jax.nn.sigmoid(acc)

        qh = conv(qbuf, 0)
        kh = conv(kbuf, 1) * (dh ** -0.5)
        vh = v_ref[:, cs]
        v_ext = jnp.concatenate([vh.astype(F32), ones_col], axis=1)

        li_col = gl[:, h:h + 1]
        b_col = csum[:, ML_HEADS + h:ML_HEADS + h + 1]
        li_row = gl_t[h:h + 1, :]
        b_row = csum_t[ML_HEADS + h:ML_HEADS + h + 1, :]
        b_tot = b_row[:, L - 1:L]
        m0 = m_scr[h:h + 1, 0:1]

        a_col = b_tot - b_col + li_col
        m_loc = jnp.max(a_col, axis=0, keepdims=True)
        wa = jnp.exp(a_col - m_loc)

        qb = qh.astype(BF16)
        qk = lax.dot_general(qb, kh.astype(BF16), (((1,), (1,)), ((), ())),
                             preferred_element_type=F32)
        log_d = jnp.where(tril, b_col - b_row + li_row, -jnp.inf)
        inter_log = b_col + m0
        m_t = jnp.maximum(inter_log, jnp.max(log_d, axis=1, keepdims=True))
        s_mat = jnp.exp(log_d - m_t) * qk
        w_inter = jnp.exp(inter_log - m_t)
        state = st_scr[h]
        nd = (jnp.dot(s_mat.astype(BF16), v_ext.astype(BF16), preferred_element_type=F32)
              + w_inter * jnp.dot(qb, state.astype(BF16), preferred_element_type=F32))
        num = nd[:, :dh]
        den = nd[:, dh:dh + 1]
        hval = num / jnp.maximum(jnp.abs(den), jnp.exp(-m_t))
        gate = jax.nn.sigmoid(og_ref[:, cs].astype(F32))
        o_ref[:, cs] = (hval * gate).astype(o_ref.dtype)

        m_new = jnp.maximum(b_tot + m0, m_loc)
        s_old = jnp.exp(b_tot + m0 - m_new)
        s_new = jnp.exp(m_loc - m_new)
        upd = jnp.dot(kh.T.astype(BF16), (wa * v_ext).astype(BF16), preferred_element_type=F32)
        st_scr[h] = s_old * state + s_new * upd
        m_scr[h:h + 1, :] = jnp.broadcast_to(m_new, (1, LANES))

    qbuf[0:halo, :] = qbuf[L:L + halo, :]
    kbuf[0:halo, :] = kbuf[L:L + halo, :]


def mlstm(proj, gates, conv_w, i_bias, f_bias):
    t = proj.shape[0]
    L = ML_CHUNK
    wb = ML_WIDTH
    bias = jnp.zeros((1, LANES), F32).at[0, :ML_HEADS].set(i_bias).at[0, ML_HEADS:2 * ML_HEADS].set(f_bias)
    col = lambda off: pl.BlockSpec((L, wb), lambda c, off=off: (c, off // wb))
    return pl.pallas_call(
        _mlstm_kernel,
        out_shape=jax.ShapeDtypeStruct((t, wb), BF16),
        grid=(t // L,),
        in_specs=[col(COL_ML_Q), col(COL_ML_K), col(COL_ML_V), col(COL_ML_O),
                  pl.BlockSpec((L, LANES), lambda c: (c, 0)),
                  pl.BlockSpec((ML_CONV, 2 * wb), lambda c: (0, 0)),
                  pl.BlockSpec((1, LANES), lambda c: (0, 0))],
        out_specs=pl.BlockSpec((L, wb), lambda c: (c, 0)),
        scratch_shapes=[pltpu.VMEM((L + SUBLANES, wb), F32), pltpu.VMEM((L + SUBLANES, wb), F32),
                        pltpu.VMEM((ML_HEADS, ML_HEAD_DIM, ML_EXT), F32),
                        pltpu.VMEM((SUBLANES, LANES), F32)],
        compiler_params=_params(("arbitrary",)),
        name="mlstm",
    )(proj, proj, proj, proj, gates, conv_w, bias)


def _merge_kernel(ysb_ref, ys5_ref, yml_ref, psb_ref, ps5_ref, pml_ref,
                  gsb_ref, gs5_ref, gml_ref, o_ref):
    def branch(y_ref, p_ref, g_ref):
        return (jax.nn.sigmoid(g_ref[...].astype(F32))
                * jnp.dot(y_ref[...], p_ref[...], preferred_element_type=F32))

    o_ref[...] = (branch(ysb_ref, psb_ref, gsb_ref) + branch(ys5_ref, ps5_ref, gs5_ref)
                  + branch(yml_ref, pml_ref, gml_ref)).astype(o_ref.dtype)


def merge(y_sb, y_s5, y_ml, p_sb, p_s5, p_ml, proj):
    t, k = y_sb.shape
    n = p_sb.shape[1]
    tm, tn = min(1024, t), 512
    ysp = pl.BlockSpec((tm, k), lambda i, j: (i, 0))
    psp = pl.BlockSpec((k, tn), lambda i, j: (0, j))
    gsp = lambda off: pl.BlockSpec((tm, tn), lambda i, j, off=off: (i, off // tn + j))
    return pl.pallas_call(
        _merge_kernel,
        out_shape=jax.ShapeDtypeStruct((t, n), BF16),
        grid=(t // tm, n // tn),
        in_specs=[ysp, ysp, ysp, psp, psp, psp, gsp(COL_G_SB), gsp(COL_G_S5), gsp(COL_G_ML)],
        out_specs=pl.BlockSpec((tm, tn), lambda i, j: (i, j)),
        compiler_params=_params(("parallel", "parallel")),
        name="merge",
    )(y_sb, y_s5, y_ml, p_sb, p_s5, p_ml, proj, proj, proj)


def _route(logits):
    lane = lax.broadcasted_iota(jnp.int32, logits.shape, 1)
    neg = -jnp.inf
    big = jnp.int32(LANES)
    is_g = jnp.logical_and(lane >= N_EXPERTS, lane < N_EXPERTS + N_GROUPS)
    gl = jnp.where(is_g, logits, neg)
    gmax = jnp.max(gl, axis=1, keepdims=True)
    g_idx = jnp.min(jnp.where(gl == gmax, lane, big), axis=1, keepdims=True) - N_EXPERTS
    g_w = 1.0 / jnp.sum(jnp.where(is_g, jnp.exp(gl - gmax), 0.0), axis=1, keepdims=True)
    in_grp = jnp.logical_and(lane >= g_idx * EXPERTS_PER_GROUP, lane < (g_idx + 1) * EXPERTS_PER_GROUP)
    e1 = jnp.where(in_grp, logits, neg)
    v1 = jnp.max(e1, axis=1, keepdims=True)
    i1 = jnp.min(jnp.where(e1 == v1, lane, big), axis=1, keepdims=True)
    e2 = jnp.where(lane == i1, neg, e1)
    v2 = jnp.max(e2, axis=1, keepdims=True)
    i2 = jnp.min(jnp.where(e2 == v2, lane, big), axis=1, keepdims=True)
    ex = jnp.exp(v2 - v1)
    w1 = g_w / (1.0 + ex)
    w2 = g_w * ex / (1.0 + ex)
    return jnp.where(lane == i1, w1, 0.0) + jnp.where(lane == i2, w2, 0.0)


def _outproj_kernel(m_ref, w_ref, x_ref, g1_ref, ng_ref, sh_ref, sc_ref, wr_ref, br_ref,
                    xo_ref, h_ref, gate_ref):
    y = jnp.dot(m_ref[...], w_ref[...], preferred_element_type=F32)
    x = x_ref[...] + g1_ref[...] * y
    xo_ref[...] = x
    hb = _norm_mod(x, ng_ref[...], sh_ref[...], sc_ref[...]).astype(BF16)
    h_ref[...] = hb
    logits = jnp.dot(hb, wr_ref[...], preferred_element_type=F32) + br_ref[...]
    gate_ref[...] = _route(logits)


def out_proj(merged, w_out, x, gate1, norm_g, shift, scale, w_route, b_route):
    t, d = x.shape
    tm = min(256, t)
    row = pl.BlockSpec((tm, d), lambda i: (i, 0))
    vec = pl.BlockSpec((1, d), lambda i: (0, 0))
    return pl.pallas_call(
        _outproj_kernel,
        out_shape=(jax.ShapeDtypeStruct((t, d), F32), jax.ShapeDtypeStruct((t, d), BF16),
                   jax.ShapeDtypeStruct((t, LANES), F32)),
        grid=(t // tm,),
        in_specs=[row, pl.BlockSpec((d, d), lambda i: (0, 0)), row, vec, vec, vec, vec,
                  pl.BlockSpec((d, LANES), lambda i: (0, 0)), pl.BlockSpec((1, LANES), lambda i: (0, 0))],
        out_specs=(row, row, pl.BlockSpec((tm, LANES), lambda i: (i, 0))),
        compiler_params=_params(("parallel",)),
        name="out_proj",
    )(merged, w_out, x, gate1, norm_g, shift, scale, w_route, b_route)


def _moe_kernel(h_ref, gt_ref, w1_ref, w3_ref, w2_ref, x_ref, g2_ref, fg_ref, o_ref, acc, *, final_norm):
    e = pl.program_id(1)

    @pl.when(e == 0)
    def _():
        acc[...] = jnp.zeros(acc.shape, F32)

    hb = h_ref[...]
    a = jnp.dot(hb, w1_ref[...], preferred_element_type=F32)
    b = jnp.dot(hb, w3_ref[...], preferred_element_type=F32)
    gt = gt_ref[...]
    lane = lax.broadcasted_iota(jnp.int32, gt.shape, 1)
    ge = jnp.sum(jnp.where(lane == e, gt, 0.0), axis=1, keepdims=True)
    hid = a * jax.nn.sigmoid(a) * b * ge
    acc[...] += jnp.dot(hid.astype(BF16), w2_ref[...], preferred_element_type=F32)

    @pl.when(e == pl.num_programs(1) - 1)
    def _():
        x = x_ref[...] + g2_ref[...] * acc[...]
        if final_norm:
            ms = jnp.mean(x * x, axis=-1, keepdims=True)
            x = x * lax.rsqrt(ms + EPS) * fg_ref[...]
        o_ref[...] = x


def moe(h, gates, w1, w3, w2, x, gate2, final_g, final_norm):
    t, d = x.shape
    ne, _, ff = w1.shape
    tm = min(512, t)
    row = pl.BlockSpec((tm, d), lambda i, e: (i, 0))
    vec = pl.BlockSpec((1, d), lambda i, e: (0, 0))
    return pl.pallas_call(
        functools.partial(_moe_kernel, final_norm=final_norm),
        out_shape=jax.ShapeDtypeStruct((t, d), F32),
        grid=(t // tm, ne),
        in_specs=[row, pl.BlockSpec((tm, LANES), lambda i, e: (i, 0)),
                  pl.BlockSpec((None, d, ff), lambda i, e: (e, 0, 0)),
                  pl.BlockSpec((None, d, ff), lambda i, e: (e, 0, 0)),
                  pl.BlockSpec((None, ff, d), lambda i, e: (e, 0, 0)),
                  row, vec, vec],
        out_specs=row,
        scratch_shapes=[pltpu.VMEM((tm, d), F32)],
        compiler_params=_params(("parallel", "arbitrary")),
        name="moe",
    )(h, gates, w1, w3, w2, x, gate2, final_g)


def kernel(x, c, norm_mix_g, norm_moe_g, final_g, w_ada, b_ada, w_in, ml_conv, ml_i_bias, ml_f_bias,
           s5_lam_re, s5_lam_im, s5_log_dt, s5_b_re, s5_b_im, s5_c_re, s5_c_im, s5_d, s5_w_glu,
           p_sb, p_s5, p_ml, w_out, moe_w_group, moe_b_group, moe_w_router, moe_b_router,
           moe_w1, moe_w3, moe_w2):
    bsz, seq, d = x.shape
    assert bsz == 1 and d == D_MODEL
    assert seq % max(ML_CHUNK, SB_BLOCK, S5_CHUNK * SUBLANES) == 0
    depth = w_in.shape[0]
    xt = x.reshape(seq, d)
    mod = ada_mod(c, w_ada, b_ada)
    nc = seq // S5_CHUNK
    for l in range(depth):
        shift1, scale1, gate1, shift2, scale2, gate2 = [mod[l, :, i * d:(i + 1) * d] for i in range(6)]
        w_l = w_in[l]
        w_main = jnp.concatenate([w_l[:, :COL_GATES], w_l[:, COL_GATES + N_GATE_COLS:]], axis=1).astype(BF16)
        w_gate = jnp.pad(w_l[:, COL_GATES:COL_GATES + N_GATE_COLS], ((0, 0), (0, LANES - N_GATE_COLS))).astype(BF16)
        proj, gates = in_proj(xt, norm_mix_g[l].reshape(1, d), shift1, scale1, w_main, w_gate)

        y_sb = sb_attention(proj)

        u = proj[:, COL_S5_U:COL_S5_U + S5_WIDTH]
        u_g = u.reshape(nc, S5_CHUNK, S5_GROUPS, S5_GROUP).transpose(2, 0, 1, 3).reshape(S5_GROUPS, nc, S5_CHUNK * S5_GROUP)
        y_g = s5_scan(u_g, *s5_weights(s5_lam_re[l], s5_lam_im[l], s5_log_dt[l], s5_b_re[l], s5_b_im[l],
                                       s5_c_re[l], s5_c_im[l], s5_d[l]))
        y_act = y_g.reshape(S5_GROUPS, nc, S5_CHUNK, S5_GROUP).transpose(1, 2, 0, 3).reshape(seq, S5_WIDTH)
        y_s5 = s5_glu(y_act, s5_w_glu[l].astype(BF16))

        y_ml = mlstm(proj, gates, ml_conv[l], ml_i_bias[l], ml_f_bias[l])

        merged = merge(y_sb, y_s5, y_ml, p_sb[l].astype(BF16), p_s5[l].astype(BF16), p_ml[l].astype(BF16), proj)

        w_route = jnp.pad(jnp.concatenate([moe_w_router[l], moe_w_group[l]], axis=1),
                          ((0, 0), (0, LANES - N_EXPERTS - N_GROUPS))).astype(BF16)
        b_route = jnp.pad(jnp.concatenate([moe_b_router[l], moe_b_group[l]]),
                          (0, LANES - N_EXPERTS - N_GROUPS)).reshape(1, LANES)
        xt, h2, egates = out_proj(merged, w_out[l].astype(BF16), xt, gate1, norm_moe_g[l].reshape(1, d),
                                  shift2, scale2, w_route, b_route)
        xt = moe(h2, egates, moe_w1[l].astype(BF16), moe_w3[l].astype(BF16), moe_w2[l].astype(BF16),
                 xt, gate2, final_g.reshape(1, d), final_norm=(l == depth - 1))
    return xt.reshape(bsz, seq, d)
```

```python
import functools
import math

import jax
import jax.numpy as jnp
from jax import lax
from jax.experimental import pallas as pl
from jax.experimental.pallas import tpu as pltpu

F32 = jnp.float32
BF16 = jnp.bfloat16
HIGHEST = lax.Precision.HIGHEST

D_MODEL = 2048
DEPTH = 2
SB_HEADS = 8
SB_HEAD_DIM = 128
SB_WIDTH = SB_HEADS * SB_HEAD_DIM
S5_WIDTH = 1024
S5_GROUP = 16
S5_GROUPS = S5_WIDTH // S5_GROUP
S5_STATE = 64
ML_HEADS = 4
ML_HEAD_DIM = 256
ML_WIDTH = ML_HEADS * ML_HEAD_DIM
ML_CONV = 4
N_GROUPS = 4
EXPERTS_PER_GROUP = 8
N_EXPERTS = N_GROUPS * EXPERTS_PER_GROUP
EXPERT_FF = 256
EPS = 1e-6

LANES = 128
SUBLANES = 8
VMEM_LIMIT = 52 * 1024 * 1024

WIN_S5_U = 3072
WIN_ML_Q = 4096
WIN_GATES = 8192
N_GATE_COLS = 2 * ML_HEADS
WIN_G_SB = WIN_GATES + N_GATE_COLS
COL_SB_Q, COL_SB_K, COL_SB_V = 0, 1024, 2048
COL_ML_Q, COL_ML_K, COL_ML_V, COL_ML_O = 3072, 4096, 5120, 6144
COL_G_SB, COL_G_S5, COL_G_ML = 7168, 9216, 11264

S5_CHUNK = 64
ML_CHUNK = 128
SB_BLOCK = 256
SB_HEADS_PER_STEP = 2
MOE_TILE = 256
SB_LOG_CUTOFF = -105.0


def _params(sem, vmem=VMEM_LIMIT):
    return pltpu.CompilerParams(dimension_semantics=sem, vmem_limit_bytes=vmem)


def _log_sigmoid(z):
    return jnp.minimum(z, 0.0) - jnp.log1p(jnp.exp(-jnp.abs(z)))


def _ada_kernel(c_ref, w_ref, b_ref, o_ref):
    c = c_ref[...]
    ca = c * jax.nn.sigmoid(c)
    r = jnp.dot(ca, w_ref[...], precision=HIGHEST, preferred_element_type=F32)
    o_ref[...] = r[0:1, :] + b_ref[...]


def ada_mod(c, w_ada, b_ada):
    depth, d, n = w_ada.shape
    tn = 1024
    c8 = jnp.broadcast_to(c.reshape(1, d), (SUBLANES, d))
    return pl.pallas_call(
        _ada_kernel,
        out_shape=jax.ShapeDtypeStruct((depth, 1, n), F32),
        grid=(depth, n // tn),
        in_specs=[pl.BlockSpec((SUBLANES, d), lambda l, j: (0, 0)),
                  pl.BlockSpec((None, d, tn), lambda l, j: (l, 0, j)),
                  pl.BlockSpec((None, 1, tn), lambda l, j: (l, 0, j))],
        out_specs=pl.BlockSpec((None, 1, tn), lambda l, j: (l, 0, j)),
        compiler_params=_params(("parallel", "parallel")),
        name="ada_mod",
    )(c8, w_ada, b_ada.reshape(depth, 1, n))


def _norm_mod(x, g, shift, scale):
    ms = jnp.mean(x * x, axis=-1, keepdims=True)
    y = x * lax.rsqrt(ms + EPS) * g
    return y * (1.0 + scale) + shift


def _inproj_kernel(x_ref, g_ref, sh_ref, sc_ref, w_ref, wg_ref, wu_ref, o_ref, og_ref, ou_ref, h_scr):
    @pl.when(pl.program_id(1) == 0)
    def _():
        hb = _norm_mod(x_ref[...], g_ref[...], sh_ref[...], sc_ref[...]).astype(BF16)
        h_scr[...] = hb
        og_ref[...] = jnp.dot(hb, wg_ref[...], preferred_element_type=F32)
        ou_ref[...] = lax.dot_general(wu_ref[...], hb, (((1,), (1,)), ((), ())),
                                      preferred_element_type=F32).astype(ou_ref.dtype)

    o_ref[...] = jnp.dot(h_scr[...], w_ref[...], preferred_element_type=F32).astype(o_ref.dtype)


def in_proj(x, g, shift, scale, w_main, w_gate, w_s5_t):
    t, d = x.shape
    n = w_main.shape[1]
    nu = w_s5_t.shape[0]
    tm = min(1024, t)
    tn = 512
    vec = pl.BlockSpec((1, d), lambda i, j: (0, 0))
    return pl.pallas_call(
        _inproj_kernel,
        out_shape=(jax.ShapeDtypeStruct((t, n), BF16), jax.ShapeDtypeStruct((t, LANES), F32),
                   jax.ShapeDtypeStruct((nu, t), BF16)),
        grid=(t // tm, n // tn),
        in_specs=[pl.BlockSpec((tm, d), lambda i, j: (i, 0)), vec, vec, vec,
                  pl.BlockSpec((d, tn), lambda i, j: (0, j)),
                  pl.BlockSpec((d, LANES), lambda i, j: (0, 0)),
                  pl.BlockSpec((nu, d), lambda i, j: (0, 0))],
        out_specs=(pl.BlockSpec((tm, tn), lambda i, j: (i, j)),
                   pl.BlockSpec((tm, LANES), lambda i, j: (i, 0)),
                   pl.BlockSpec((nu, tm), lambda i, j: (0, i))),
        scratch_shapes=[pltpu.VMEM((tm, d), BF16)],
        compiler_params=_params(("parallel", "arbitrary")),
        name="in_proj",
    )(x, g, shift, scale, w_main, w_gate, w_s5_t)


def _sb_kernel(q_ref, k_ref, v_ref, o_ref, acc_ref, run_ref):
    blk = SB_BLOCK
    hd = SB_HEAD_DIM
    i = pl.program_id(1)
    scale = hd ** -0.5
    row = lax.broadcasted_iota(jnp.int32, (blk, blk), 0)
    col = lax.broadcasted_iota(jnp.int32, (blk, blk), 1)
    after = jnp.where(row > col, 1.0, 0.0).astype(BF16)
    causal = col < row
    acc_ref[...] = jnp.zeros(acc_ref.shape, F32)
    run_ref[...] = jnp.zeros(run_ref.shape, F32)

    def step(kb, diagonal):
        start = pl.multiple_of(kb * blk, blk)
        worst = None
        for h in range(SB_HEADS_PER_STEP):
            cs = slice(h * hd, (h + 1) * hd)
            z = lax.dot_general(q_ref[:, cs], k_ref[pl.ds(start, blk), cs], (((1,), (1,)), ((), ())),
                                preferred_element_type=F32) * scale
            softplus = jnp.maximum(z, 0.0) + jnp.log(1.0 + jnp.exp(-jnp.abs(z)))
            log_keep = -softplus
            log_beta = z - softplus
            if diagonal:
                log_keep = jnp.where(causal, log_keep, 0.0)
            hi = log_keep.astype(BF16)
            lo = (log_keep - hi.astype(F32)).astype(BF16)
            later = (jnp.dot(hi, after, preferred_element_type=F32)
                     + jnp.dot(lo, after, preferred_element_type=F32))
            run = run_ref[h]
            w = jnp.exp(log_beta + later + run)
            if diagonal:
                w = jnp.where(causal, w, 0.0)
            acc_ref[h] += jnp.dot(w.astype(BF16), v_ref[pl.ds(start, blk), cs], preferred_element_type=F32)
            run = run + jnp.sum(log_keep, axis=1, keepdims=True)
            run_ref[h] = run
            top = jnp.max(run)
            worst = top if worst is None else jnp.maximum(worst, top)
        return worst

    worst = step(i, True)

    def cond(carry):
        kb, worst = carry
        return jnp.logical_and(kb >= 0, worst > SB_LOG_CUTOFF)

    def body(carry):
        kb, _ = carry
        return kb - 1, step(kb, False)

    lax.while_loop(cond, body, (i - 1, worst))
    for h in range(SB_HEADS_PER_STEP):
        o_ref[:, h * hd:(h + 1) * hd] = acc_ref[h].astype(o_ref.dtype)


def sb_attention(proj):
    t = proj.shape[0]
    blk = SB_BLOCK
    wide = SB_HEADS_PER_STEP * SB_HEAD_DIM
    qb, kb, vb = COL_SB_Q // wide, COL_SB_K // wide, COL_SB_V // wide
    return pl.pallas_call(
        _sb_kernel,
        out_shape=jax.ShapeDtypeStruct((t, SB_WIDTH), BF16),
        grid=(SB_HEADS // SB_HEADS_PER_STEP, t // blk),
        in_specs=[pl.BlockSpec((blk, wide), lambda h, i: (i, qb + h)),
                  pl.BlockSpec((t, wide), lambda h, i: (0, kb + h)),
                  pl.BlockSpec((t, wide), lambda h, i: (0, vb + h))],
        out_specs=pl.BlockSpec((blk, wide), lambda h, i: (i, h)),
        scratch_shapes=[pltpu.VMEM((SB_HEADS_PER_STEP, blk, SB_HEAD_DIM), F32),
                        pltpu.VMEM((SB_HEADS_PER_STEP, blk, 1), F32)],
        compiler_params=_params(("parallel", "arbitrary")),
        name="sb_attention",
    )(proj, proj, proj)


def s5_weights(lam_re, lam_im, log_dt, b_re, b_im, c_re, c_im, d_skip):
    lc, g, p, hh = S5_CHUNK, S5_GROUPS, S5_STATE, S5_GROUP
    dt = jnp.exp(log_dt)[:, None]
    zr, zi = lam_re * dt, lam_im * dt
    e1 = jnp.exp(zr)
    lbr, lbi = e1 * jnp.cos(zi), e1 * jnp.sin(zi)
    den = lam_re * lam_re + lam_im * lam_im
    fr = ((lbr - 1.0) * lam_re + lbi * lam_im) / den
    fi = (lbi * lam_re - (lbr - 1.0) * lam_im) / den
    bbr = fr[:, :, None] * b_re - fi[:, :, None] * b_im
    bbi = fr[:, :, None] * b_im + fi[:, :, None] * b_re
    tau = jnp.arange(lc + 1, dtype=F32)[None, :, None]
    mag = jnp.exp(zr[:, None, :] * tau)
    pwr = mag * jnp.cos(zi[:, None, :] * tau)
    pwi = mag * jnp.sin(zi[:, None, :] * tau)
    cpr = c_re[:, None] * pwr[:, :, None, :] - c_im[:, None] * pwi[:, :, None, :]
    cpi = c_re[:, None] * pwi[:, :, None, :] + c_im[:, None] * pwr[:, :, None, :]
    kern = (jnp.einsum('gthp,gpk->gthk', cpr[:, :lc], bbr, precision=HIGHEST)
            - jnp.einsum('gthp,gpk->gthk', cpi[:, :lc], bbi, precision=HIGHEST))
    kz = jnp.pad(kern.transpose(0, 3, 2, 1), ((0, 0), (0, 0), (0, 0), (0, lc)))
    n = hh * 2 * lc
    flat = jnp.pad(kz.reshape(g, hh, n), ((0, 0), (0, 0), (0, 1)))
    skew = jnp.tile(flat, (1, 1, lc))[:, :, :lc * n].reshape(g, hh, lc, hh, 2 * lc)
    w_intra = skew[..., :lc].reshape(g, hh * lc, hh * lc).astype(BF16)
    rev_r, rev_i = pwr[:, lc - 1::-1][:, None, :lc], pwi[:, lc - 1::-1][:, None, :lc]
    bt_r, bt_i = bbr.transpose(0, 2, 1)[:, :, None, :], bbi.transpose(0, 2, 1)[:, :, None, :]
    w_state = jnp.concatenate([rev_r * bt_r - rev_i * bt_i, rev_r * bt_i + rev_i * bt_r],
                              axis=-1).reshape(g, hh * lc, 2 * p).astype(BF16)
    wor = cpr[:, 1:].transpose(0, 3, 2, 1).reshape(g, p, hh * lc)
    woi = -cpi[:, 1:].transpose(0, 3, 2, 1).reshape(g, p, hh * lc)
    w_out = jnp.concatenate([wor, woi], axis=1).astype(BF16)
    a_chunk = jnp.stack([pwr[:, lc], pwi[:, lc]], axis=1)
    d_row = jnp.broadcast_to(d_skip.reshape(g, hh, 1), (g, hh, lc)).reshape(g, 1, hh * lc)
    return w_intra, w_state, w_out, a_chunk, d_row


def _s5_kernel(u_ref, wi_ref, ws_ref, wo_ref, a_ref, d_ref, o_ref, sr_scr, si_scr, xr_scr, xi_scr):
    p = S5_STATE
    lc = S5_CHUNK
    nc = u_ref.shape[1]
    uf = jnp.concatenate([u_ref[h].astype(F32) for h in range(S5_GROUP)], axis=1)
    u = uf.astype(BF16)
    s = jnp.dot(u, ws_ref[...], preferred_element_type=F32)
    sr_scr[...] = s[:, :p]
    si_scr[...] = s[:, p:]
    ar = a_ref[0:1, :]
    ai = a_ref[1:2, :]

    def body(c, carry):
        xr, xi = carry
        xr_scr[pl.ds(c, 1), :] = xr
        xi_scr[pl.ds(c, 1), :] = xi
        sr = sr_scr[pl.ds(c, 1), :]
        si = si_scr[pl.ds(c, 1), :]
        return ar * xr - ai * xi + sr, ar * xi + ai * xr + si

    zero = jnp.zeros((1, p), F32)
    lax.fori_loop(0, nc, body, (zero, zero))
    y = jnp.dot(u, wi_ref[...], preferred_element_type=F32)
    y = y + jnp.dot(xr_scr[...].astype(BF16), wo_ref[0:p, :], preferred_element_type=F32)
    y = y + jnp.dot(xi_scr[...].astype(BF16), wo_ref[p:2 * p, :], preferred_element_type=F32)
    y = jax.nn.gelu(y + d_ref[...] * uf)
    for h in range(S5_GROUP):
        o_ref[h] = y[:, h * lc:(h + 1) * lc].astype(o_ref.dtype)


def s5_scan(u_t, w_intra, w_state, w_out, a_chunk, d_row):
    width, t = u_t.shape
    g, hh, lc, p = S5_GROUPS, S5_GROUP, S5_CHUNK, S5_STATE
    nc = t // lc
    grp = lambda n0, n1: pl.BlockSpec((None, n0, n1), lambda i: (i, 0, 0))
    chan = pl.BlockSpec((None, hh, nc, lc), lambda i: (i, 0, 0, 0))
    out = pl.pallas_call(
        _s5_kernel,
        out_shape=jax.ShapeDtypeStruct((g, hh, nc, lc), BF16),
        grid=(g,),
        in_specs=[chan, grp(hh * lc, hh * lc), grp(hh * lc, 2 * p), grp(2 * p, hh * lc),
                  grp(2, p), grp(1, hh * lc)],
        out_specs=chan,
        scratch_shapes=[pltpu.VMEM((nc, p), F32)] * 4,
        compiler_params=_params(("parallel",)),
        name="s5_scan",
    )(u_t.reshape(g, hh, nc, lc), w_intra, w_state, w_out, a_chunk, d_row)
    return out.reshape(width, t)


def _glu_kernel(y_ref, w_ref, o_ref):
    y = y_ref[...]
    a = jnp.dot(w_ref[...], y, preferred_element_type=F32)
    o_ref[...] = (y.astype(F32) * jax.nn.sigmoid(a)).T.astype(o_ref.dtype)


def s5_glu(y_t, w_glu_t):
    n, t = y_t.shape
    tn = min(1024, t)
    return pl.pallas_call(
        _glu_kernel,
        out_shape=jax.ShapeDtypeStruct((t, n), BF16),
        grid=(t // tn,),
        in_specs=[pl.BlockSpec((n, tn), lambda i: (0, i)),
                  pl.BlockSpec((n, n), lambda i: (0, 0))],
        out_specs=pl.BlockSpec((tn, n), lambda i: (i, 0)),
        compiler_params=_params(("parallel",)),
        name="s5_glu",
    )(y_t, w_glu_t)


ML_EXT = ML_HEAD_DIM + LANES


def _mlstm_kernel(q_ref, k_ref, v_ref, og_ref, g_ref, cw_ref, bias_ref, o_ref,
                  qbuf, kbuf, st_scr, m_scr):
    L = ML_CHUNK
    dh = ML_HEAD_DIM
    halo = SUBLANES
    c = pl.program_id(0)

    @pl.when(c == 0)
    def _():
        qbuf[0:halo, :] = jnp.zeros((halo, ML_WIDTH), F32)
        kbuf[0:halo, :] = jnp.zeros((halo, ML_WIDTH), F32)
        st_scr[...] = jnp.zeros(st_scr.shape, F32)
        m_scr[...] = jnp.zeros(m_scr.shape, F32)

    qbuf[halo:halo + L, :] = q_ref[...].astype(F32)
    kbuf[halo:halo + L, :] = k_ref[...].astype(F32)

    gp = g_ref[...] + bias_ref[...]
    lane = lax.broadcasted_iota(jnp.int32, gp.shape, 1)
    gl = jnp.where(lane < ML_HEADS, gp, _log_sigmoid(gp))
    r_i = lax.broadcasted_iota(jnp.int32, (L, L), 0)
    c_i = lax.broadcasted_iota(jnp.int32, (L, L), 1)
    tril = r_i >= c_i
    csum = jnp.dot(jnp.where(tril, 1.0, 0.0), gl, precision=HIGHEST,
                   preferred_element_type=F32)
    gl_t = gl.T
    csum_t = csum.T
    ones_col = (lax.broadcasted_iota(jnp.int32, (L, LANES), 1) == 0).astype(F32)

    for h in range(ML_HEADS):
        cs = slice(h * dh, (h + 1) * dh)

        def conv(buf, part):
            acc = None
            for j in range(ML_CONV):
                off = halo - (ML_CONV - 1) + j
                term = buf[off:off + L, cs] * cw_ref[j:j + 1, part * ML_WIDTH + h * dh:part * ML_WIDTH + (h + 1) * dh]
                acc = term if acc is None else acc + term
            return acc * jax.nn.sigmoid(acc)

        qh = conv(qbuf, 0)
        kh = conv(kbuf, 1) * (dh ** -0.5)
        vh = v_ref[:, cs]
        v_ext = jnp.concatenate([vh.astype(F32), ones_col], axis=1)

        li_col = gl[:, h:h + 1]
        b_col = csum[:, ML_HEADS + h:ML_HEADS + h + 1]
        li_row = gl_t[h:h + 1, :]
        b_row = csum_t[ML_HEADS + h:ML_HEADS + h + 1, :]
        b_tot = b_row[:, L - 1:L]
        m0 = m_scr[h:h + 1, 0:1]

        a_col = b_tot - b_col + li_col
        m_loc = jnp.max(a_col, axis=0, keepdims=True)
        wa = jnp.exp(a_col - m_loc)

        qb = qh.astype(BF16)
        qk = lax.dot_general(qb, kh.astype(BF16), (((1,), (1,)), ((), ())),
                             preferred_element_type=F32)
        log_d = jnp.where(tril, b_col - b_row + li_row, -jnp.inf)
        inter_log = b_col + m0
        m_t = jnp.maximum(inter_log, jnp.max(log_d, axis=1, keepdims=True))
        s_mat = jnp.exp(log_d - m_t) * qk
        w_inter = jnp.exp(inter_log - m_t)
        state = st_scr[h]
        nd = (jnp.dot(s_mat.astype(BF16), v_ext.astype(BF16), preferred_element_type=F32)
              + w_inter * jnp.dot(qb, state.astype(BF16), preferred_element_type=F32))
        num = nd[:, :dh]
        den = nd[:, dh:dh + 1]
        hval = num / jnp.maximum(jnp.abs(den), jnp.exp(-m_t))
        gate = jax.nn.sigmoid(og_ref[:, cs].astype(F32))
        o_ref[:, cs] = (hval * gate).astype(o_ref.dtype)

        m_new = jnp.maximum(b_tot + m0, m_loc)
        s_old = jnp.exp(b_tot + m0 - m_new)
        s_new = jnp.exp(m_loc - m_new)
        upd = jnp.dot(kh.T.astype(BF16), (wa * v_ext).astype(BF16), preferred_element_type=F32)
        st_scr[h] = s_old * state + s_new * upd
        m_scr[h:h + 1, :] = jnp.broadcast_to(m_new, (1, LANES))

    qbuf[0:halo, :] = qbuf[L:L + halo, :]
    kbuf[0:halo, :] = kbuf[L:L + halo, :]


def mlstm(proj, gates, conv_w, i_bias, f_bias):
    t = proj.shape[0]
    L = ML_CHUNK
    wb = ML_WIDTH
    bias = jnp.zeros((1, LANES), F32).at[0, :ML_HEADS].set(i_bias).at[0, ML_HEADS:2 * ML_HEADS].set(f_bias)
    col = lambda off: pl.BlockSpec((L, wb), lambda c, off=off: (c, off // wb))
    return pl.pallas_call(
        _mlstm_kernel,
        out_shape=jax.ShapeDtypeStruct((t, wb), BF16),
        grid=(t // L,),
        in_specs=[col(COL_ML_Q), col(COL_ML_K), col(COL_ML_V), col(COL_ML_O),
                  pl.BlockSpec((L, LANES), lambda c: (c, 0)),
                  pl.BlockSpec((ML_CONV, 2 * wb), lambda c: (0, 0)),
                  pl.BlockSpec((1, LANES), lambda c: (0, 0))],
        out_specs=pl.BlockSpec((L, wb), lambda c: (c, 0)),
        scratch_shapes=[pltpu.VMEM((L + SUBLANES, wb), F32), pltpu.VMEM((L + SUBLANES, wb), F32),
                        pltpu.VMEM((ML_HEADS, ML_HEAD_DIM, ML_EXT), F32),
                        pltpu.VMEM((SUBLANES, LANES), F32)],
        compiler_params=_params(("arbitrary",)),
        name="mlstm",
    )(proj, proj, proj, proj, gates, conv_w, bias)


def _merge_kernel(ysb_ref, ys5_ref, yml_ref, psb_ref, ps5_ref, pml_ref,
                  gsb_ref, gs5_ref, gml_ref, o_ref):
    def branch(y_ref, p_ref, g_ref):
        return (jax.nn.sigmoid(g_ref[...].astype(F32))
                * jnp.dot(y_ref[...], p_ref[...], preferred_element_type=F32))

    o_ref[...] = (branch(ysb_ref, psb_ref, gsb_ref) + branch(ys5_ref, ps5_ref, gs5_ref)
                  + branch(yml_ref, pml_ref, gml_ref)).astype(o_ref.dtype)


def merge(y_sb, y_s5, y_ml, p_sb, p_s5, p_ml, proj):
    t, k = y_sb.shape
    n = p_sb.shape[1]
    tm, tn = min(1024, t), 512
    ysp = pl.BlockSpec((tm, k), lambda i, j: (i, 0))
    psp = pl.BlockSpec((k, tn), lambda i, j: (0, j))
    gsp = lambda off: pl.BlockSpec((tm, tn), lambda i, j, off=off: (i, off // tn + j))
    return pl.pallas_call(
        _merge_kernel,
        out_shape=jax.ShapeDtypeStruct((t, n), BF16),
        grid=(t // tm, n // tn),
        in_specs=[ysp, ysp, ysp, psp, psp, psp, gsp(COL_G_SB), gsp(COL_G_S5), gsp(COL_G_ML)],
        out_specs=pl.BlockSpec((tm, tn), lambda i, j: (i, j)),
        compiler_params=_params(("parallel", "parallel")),
        name="merge",
    )(y_sb, y_s5, y_ml, p_sb, p_s5, p_ml, proj, proj, proj)


def _route(logits):
    lane = lax.broadcasted_iota(jnp.int32, logits.shape, 1)
    neg = -jnp.inf
    big = jnp.int32(LANES)
    is_g = jnp.logical_and(lane >= N_EXPERTS, lane < N_EXPERTS + N_GROUPS)
    gl = jnp.where(is_g, logits, neg)
    gmax = jnp.max(gl, axis=1, keepdims=True)
    g_idx = jnp.min(jnp.where(gl == gmax, lane, big), axis=1, keepdims=True) - N_EXPERTS
    g_w = 1.0 / jnp.sum(jnp.where(is_g, jnp.exp(gl - gmax), 0.0), axis=1, keepdims=True)
    in_grp = jnp.logical_and(lane >= g_idx * EXPERTS_PER_GROUP, lane < (g_idx + 1) * EXPERTS_PER_GROUP)
    e1 = jnp.where(in_grp, logits, neg)
    v1 = jnp.max(e1, axis=1, keepdims=True)
    i1 = jnp.min(jnp.where(e1 == v1, lane, big), axis=1, keepdims=True)
    e2 = jnp.where(lane == i1, neg, e1)
    v2 = jnp.max(e2, axis=1, keepdims=True)
    i2 = jnp.min(jnp.where(e2 == v2, lane, big), axis=1, keepdims=True)
    ex = jnp.exp(v2 - v1)
    return i1, i2, g_w / (1.0 + ex), g_w * ex / (1.0 + ex)


META_E1, META_E2, META_W1, META_W2, META_POS1, META_POS2 = range(6)


def _outproj_kernel(m_ref, w_ref, x_ref, g1_ref, ng_ref, sh_ref, sc_ref, wr_ref, br_ref,
                    xo_ref, h_ref, meta_ref, cnt_ref, carry):
    @pl.when(pl.program_id(0) == 0)
    def _():
        carry[...] = jnp.zeros(carry.shape, F32)

    y = jnp.dot(m_ref[...], w_ref[...], preferred_element_type=F32)
    x = x_ref[...] + g1_ref[...] * y
    xo_ref[...] = x
    h = _norm_mod(x, ng_ref[...], sh_ref[...], sc_ref[...])
    h_ref[...] = h
    logits = jnp.dot(h.astype(BF16), wr_ref[...], preferred_element_type=F32) + br_ref[...]
    i1, i2, w1, w2 = _route(logits)

    tm = logits.shape[0]
    lane = lax.broadcasted_iota(jnp.int32, logits.shape, 1)
    picked = jnp.logical_or(lane == i1, lane == i2)
    r_i = lax.broadcasted_iota(jnp.int32, (tm, tm), 0)
    c_i = lax.broadcasted_iota(jnp.int32, (tm, tm), 1)
    earlier = jnp.where(r_i > c_i, 1.0, 0.0).astype(BF16)
    before = jnp.dot(earlier, jnp.where(picked, 1.0, 0.0).astype(BF16), preferred_element_type=F32)
    before = before + carry[0:1, :]
    pos1 = jnp.sum(jnp.where(lane == i1, before, 0.0), axis=1, keepdims=True)
    pos2 = jnp.sum(jnp.where(lane == i2, before, 0.0), axis=1, keepdims=True)
    total = carry[0:1, :] + jnp.sum(jnp.where(picked, 1.0, 0.0), axis=0, keepdims=True)
    carry[...] = jnp.broadcast_to(total, carry.shape)
    cnt_ref[...] = jnp.broadcast_to(total, cnt_ref.shape)

    meta = jnp.zeros(logits.shape, F32)
    for k, val in ((META_E1, i1.astype(F32)), (META_E2, i2.astype(F32)), (META_W1, w1), (META_W2, w2),
                   (META_POS1, pos1), (META_POS2, pos2)):
        meta = jnp.where(lane == k, val, meta)
    meta_ref[...] = meta


def out_proj(merged, w_out, x, gate1, norm_g, shift, scale, w_route, b_route):
    t, d = x.shape
    tm = min(256, t)
    row = pl.BlockSpec((tm, d), lambda i: (i, 0))
    vec = pl.BlockSpec((1, d), lambda i: (0, 0))
    return pl.pallas_call(
        _outproj_kernel,
        out_shape=(jax.ShapeDtypeStruct((t, d), F32), jax.ShapeDtypeStruct((t, d), F32),
                   jax.ShapeDtypeStruct((t, LANES), F32), jax.ShapeDtypeStruct((SUBLANES, LANES), F32)),
        grid=(t // tm,),
        in_specs=[row, pl.BlockSpec((d, d), lambda i: (0, 0)), row, vec, vec, vec, vec,
                  pl.BlockSpec((d, LANES), lambda i: (0, 0)), pl.BlockSpec((1, LANES), lambda i: (0, 0))],
        out_specs=(row, row, pl.BlockSpec((tm, LANES), lambda i: (i, 0)),
                   pl.BlockSpec((SUBLANES, LANES), lambda i: (0, 0))),
        scratch_shapes=[pltpu.VMEM((SUBLANES, LANES), F32)],
        compiler_params=_params(("arbitrary",)),
        name="out_proj",
    )(merged, w_out, x, gate1, norm_g, shift, scale, w_route, b_route)


def moe_plan(meta, counts, n_tiles):
    tile = MOE_TILE
    as_int = lambda k: meta[:, k].astype(jnp.int32)
    cnt = counts[0, :N_EXPERTS].astype(jnp.int32)
    padded = ((cnt + tile - 1) // tile) * tile
    seg_end = jnp.cumsum(padded)
    seg_off = seg_end - padded
    n_used = (seg_end[-1] // tile).reshape(1)
    starts = jnp.arange(n_tiles, dtype=jnp.int32) * tile
    tile_expert = jnp.sum((starts[:, None] >= seg_end[None, :]).astype(jnp.int32), axis=1)
    tile_expert = jnp.minimum(tile_expert, N_EXPERTS - 1)
    picks = (as_int(META_E1), as_int(META_E2), as_int(META_POS1), as_int(META_POS2), seg_off)
    return picks, tile_expert, n_used


def _row_copy(src, src_row, dst, dst_row, sem):
    return pltpu.make_async_copy(src.at[pl.ds(src_row, 1)], dst.at[pl.ds(dst_row, 1)], sem)


def _experts_kernel(e1_ref, e2_ref, p1_ref, p2_ref, off_ref, te_ref, nu_ref,
                    h_ref, w1_ref, w3_ref, w2_ref, y_ref, token_of, buf, sems, w1b, w3b, w2b):
    j = pl.program_id(0)
    tile = MOE_TILE
    n_used = nu_ref[0]

    @pl.when(j == 0)
    def _():
        def clear(s, carry):
            token_of[s] = 0
            return carry

        def fill(t, carry):
            token_of[off_ref[e1_ref[t]] + p1_ref[t]] = t
            token_of[off_ref[e2_ref[t]] + p2_ref[t]] = t
            return carry

        lax.fori_loop(0, token_of.shape[0], clear, 0, unroll=8)
        lax.fori_loop(0, e1_ref.shape[0], fill, 0, unroll=4)

    def gather(tile_idx, slot):
        base = tile_idx * tile

        def issue(r, carry):
            _row_copy(h_ref, token_of[base + r], buf.at[slot], r, sems.at[slot]).start()
            return carry

        lax.fori_loop(0, tile, issue, 0, unroll=4)

    @pl.when(j == 0)
    def _():
        gather(0, 0)

    @pl.when(j + 1 < n_used)
    def _():
        gather(j + 1, (j + 1) % 2)

    @pl.when(j < n_used)
    def _():
        @pl.when(jnp.logical_or(j == 0, te_ref[j] != te_ref[jnp.maximum(j - 1, 0)]))
        def _():
            w1b[...] = w1_ref[...].astype(BF16)
            w3b[...] = w3_ref[...].astype(BF16)
            w2b[...] = w2_ref[...].astype(BF16)

        slot = j % 2
        pltpu.make_async_copy(h_ref.at[pl.ds(0, tile)], buf.at[slot], sems.at[slot]).wait()
        x = buf[slot].astype(BF16)
        a = jnp.dot(x, w1b[...], preferred_element_type=F32)
        b = jnp.dot(x, w3b[...], preferred_element_type=F32)
        hid = (a * jax.nn.sigmoid(a) * b).astype(BF16)
        y_ref[...] = jnp.dot(hid, w2b[...], preferred_element_type=F32)

    @pl.when(j >= n_used)
    def _():
        y_ref[...] = jnp.zeros(y_ref.shape, y_ref.dtype)


def moe_experts(h, picks, tile_expert, n_used, w1, w3, w2, n_slots):
    t, d = h.shape
    ne, _, ff = w1.shape
    tile = MOE_TILE
    n_pref = len(picks) + 2

    def wspec(r, c):
        def index(j, *pref):
            te, nu = pref[-2], pref[-1]
            return (te[jnp.minimum(j, nu[0] - 1)], 0, 0)
        return pl.BlockSpec((None, r, c), index)

    return pl.pallas_call(
        _experts_kernel,
        out_shape=jax.ShapeDtypeStruct((n_slots, d), F32),
        grid_spec=pltpu.PrefetchScalarGridSpec(
            num_scalar_prefetch=n_pref,
            grid=(n_slots // tile,),
            in_specs=[pl.BlockSpec(memory_space=pl.ANY), wspec(d, ff), wspec(d, ff), wspec(ff, d)],
            out_specs=pl.BlockSpec((tile, d), lambda j, *pref: (j, 0)),
            scratch_shapes=[pltpu.SMEM((n_slots,), jnp.int32), pltpu.VMEM((2, tile, d), F32),
                            pltpu.SemaphoreType.DMA((2,)),
                            pltpu.VMEM((d, ff), BF16), pltpu.VMEM((d, ff), BF16), pltpu.VMEM((ff, d), BF16)]),
        compiler_params=_params(("arbitrary",)),
        name="moe_experts",
    )(*picks, tile_expert, n_used, h, w1, w3, w2)


def _combine_kernel(e1_ref, e2_ref, p1_ref, p2_ref, off_ref, y_ref, x_ref, meta_ref, g2_ref, fg_ref,
                    o_ref, buf, sems, *, final_norm):
    i = pl.program_id(0)
    tm = x_ref.shape[0]

    def gather(tile_idx, slot):
        base = tile_idx * tm

        def issue(r, carry):
            t = base + r
            _row_copy(y_ref, off_ref[e1_ref[t]] + p1_ref[t], buf.at[slot, 0], r, sems.at[slot]).start()
            _row_copy(y_ref, off_ref[e2_ref[t]] + p2_ref[t], buf.at[slot, 1], r, sems.at[slot]).start()
            return carry

        lax.fori_loop(0, tm, issue, 0, unroll=4)

    @pl.when(i == 0)
    def _():
        gather(0, 0)

    @pl.when(i + 1 < pl.num_programs(0))
    def _():
        gather(i + 1, (i + 1) % 2)

    slot = i % 2
    for k in range(2):
        pltpu.make_async_copy(y_ref.at[pl.ds(0, tm)], buf.at[slot, k], sems.at[slot]).wait()

    meta = meta_ref[...]
    mix = meta[:, META_W1:META_W1 + 1] * buf[slot, 0] + meta[:, META_W2:META_W2 + 1] * buf[slot, 1]
    x = x_ref[...] + g2_ref[...] * mix
    if final_norm:
        ms = jnp.mean(x * x, axis=-1, keepdims=True)
        x = x * lax.rsqrt(ms + EPS) * fg_ref[...]
    o_ref[...] = x


def moe_combine(ys, picks, x, meta, gate2, final_g, final_norm):
    t, d = x.shape
    tm = min(256, t)
    row = pl.BlockSpec((tm, d), lambda i, *pref: (i, 0))
    vec = pl.BlockSpec((1, d), lambda i, *pref: (0, 0))
    return pl.pallas_call(
        functools.partial(_combine_kernel, final_norm=final_norm),
        out_shape=jax.ShapeDtypeStruct((t, d), F32),
        grid_spec=pltpu.PrefetchScalarGridSpec(
            num_scalar_prefetch=len(picks),
            grid=(t // tm,),
            in_specs=[pl.BlockSpec(memory_space=pl.ANY), row,
                      pl.BlockSpec((tm, LANES), lambda i, *pref: (i, 0)), vec, vec],
            out_specs=row,
            scratch_shapes=[pltpu.VMEM((2, 2, tm, d), F32), pltpu.SemaphoreType.DMA((2,))]),
        compiler_params=_params(("arbitrary",)),
        name="moe_combine",
    )(*picks, ys, x, meta, gate2, final_g)


def kernel(x, c, norm_mix_g, norm_moe_g, final_g, w_ada, b_ada, w_in, ml_conv, ml_i_bias, ml_f_bias,
           s5_lam_re, s5_lam_im, s5_log_dt, s5_b_re, s5_b_im, s5_c_re, s5_c_im, s5_d, s5_w_glu,
           p_sb, p_s5, p_ml, w_out, moe_w_group, moe_b_group, moe_w_router, moe_b_router,
           moe_w1, moe_w3, moe_w2):
    bsz, seq, d = x.shape
    assert bsz == 1 and d == D_MODEL
    assert seq % max(ML_CHUNK, SB_BLOCK, S5_CHUNK * SUBLANES) == 0
    depth = w_in.shape[0]
    xt = x.reshape(seq, d)
    mod = ada_mod(c, w_ada, b_ada)
    for l in range(depth):
        shift1, scale1, gate1, shift2, scale2, gate2 = [mod[l, :, i * d:(i + 1) * d] for i in range(6)]
        w_l = w_in[l]
        w_main = jnp.concatenate([w_l[:, :WIN_S5_U], w_l[:, WIN_ML_Q:WIN_GATES], w_l[:, WIN_G_SB:]],
                                 axis=1).astype(BF16)
        w_gate = jnp.pad(w_l[:, WIN_GATES:WIN_G_SB], ((0, 0), (0, LANES - N_GATE_COLS))).astype(BF16)
        w_s5_t = w_l[:, WIN_S5_U:WIN_ML_Q].T.astype(BF16)
        proj, gates, u_t = in_proj(xt, norm_mix_g[l].reshape(1, d), shift1, scale1, w_main, w_gate, w_s5_t)

        y_sb = sb_attention(proj)

        y_t = s5_scan(u_t, *s5_weights(s5_lam_re[l], s5_lam_im[l], s5_log_dt[l], s5_b_re[l], s5_b_im[l],
                                       s5_c_re[l], s5_c_im[l], s5_d[l]))
        y_s5 = s5_glu(y_t, s5_w_glu[l].T.astype(BF16))

        y_ml = mlstm(proj, gates, ml_conv[l], ml_i_bias[l], ml_f_bias[l])

        merged = merge(y_sb, y_s5, y_ml, p_sb[l].astype(BF16), p_s5[l].astype(BF16), p_ml[l].astype(BF16), proj)

        w_route = jnp.pad(jnp.concatenate([moe_w_router[l], moe_w_group[l]], axis=1),
                          ((0, 0), (0, LANES - N_EXPERTS - N_GROUPS))).astype(BF16)
        b_route = jnp.pad(jnp.concatenate([moe_b_router[l], moe_b_group[l]]),
                          (0, LANES - N_EXPERTS - N_GROUPS)).reshape(1, LANES)
        xt, h2, meta, counts = out_proj(merged, w_out[l].astype(BF16), xt, gate1, norm_moe_g[l].reshape(1, d),
                                        shift2, scale2, w_route, b_route)
        n_slots = 2 * seq + N_EXPERTS * MOE_TILE
        picks, tile_expert, n_used = moe_plan(meta, counts, n_slots // MOE_TILE)
        ys = moe_experts(h2, picks, tile_expert, n_used, moe_w1[l], moe_w3[l], moe_w2[l], n_slots)
        xt = moe_combine(ys, picks, xt, meta, gate2, final_g.reshape(1, d), final_norm=(l == depth - 1))
    return xt.reshape(bsz, seq, d)
```

```python
import functools
import math

import jax
import jax.numpy as jnp
from jax import lax
from jax.experimental import pallas as pl
from jax.experimental.pallas import tpu as pltpu

F32 = jnp.float32
BF16 = jnp.bfloat16
HIGHEST = lax.Precision.HIGHEST

D_MODEL = 2048
DEPTH = 2
SB_HEADS = 8
SB_HEAD_DIM = 128
SB_WIDTH = SB_HEADS * SB_HEAD_DIM
S5_WIDTH = 1024
S5_GROUP = 16
S5_GROUPS = S5_WIDTH // S5_GROUP
S5_STATE = 64
ML_HEADS = 4
ML_HEAD_DIM = 256
ML_WIDTH = ML_HEADS * ML_HEAD_DIM
ML_CONV = 4
N_GROUPS = 4
EXPERTS_PER_GROUP = 8
N_EXPERTS = N_GROUPS * EXPERTS_PER_GROUP
EXPERT_FF = 256
EPS = 1e-6

LANES = 128
SUBLANES = 8
VMEM_LIMIT = 52 * 1024 * 1024

WIN_S5_U = 3072
WIN_ML_Q = 4096
WIN_GATES = 8192
N_GATE_COLS = 2 * ML_HEADS
WIN_G_SB = WIN_GATES + N_GATE_COLS
COL_SB_Q, COL_SB_K, COL_SB_V = 0, 1024, 2048
COL_ML_Q, COL_ML_K, COL_ML_V, COL_ML_O = 3072, 4096, 5120, 6144
COL_G_SB, COL_G_S5, COL_G_ML = 7168, 9216, 11264

S5_CHUNK = 64
ML_CHUNK = 128
SB_BLOCK = 256
SB_HEADS_PER_STEP = 2
MOE_TILE = 256
SB_LOG_CUTOFF = -105.0


def _params(sem, vmem=VMEM_LIMIT):
    return pltpu.CompilerParams(dimension_semantics=sem, vmem_limit_bytes=vmem)


def _log_sigmoid(z):
    return jnp.minimum(z, 0.0) - jnp.log1p(jnp.exp(-jnp.abs(z)))


def _ada_kernel(c_ref, w_ref, b_ref, o_ref):
    c = c_ref[...]
    ca = c * jax.nn.sigmoid(c)
    r = jnp.dot(ca, w_ref[...], precision=HIGHEST, preferred_element_type=F32)
    o_ref[...] = r[0:1, :] + b_ref[...]


def ada_mod(c, w_ada, b_ada):
    depth, d, n = w_ada.shape
    tn = 1024
    c8 = jnp.broadcast_to(c.reshape(1, d), (SUBLANES, d))
    return pl.pallas_call(
        _ada_kernel,
        out_shape=jax.ShapeDtypeStruct((depth, 1, n), F32),
        grid=(depth, n // tn),
        in_specs=[pl.BlockSpec((SUBLANES, d), lambda l, j: (0, 0)),
                  pl.BlockSpec((None, d, tn), lambda l, j: (l, 0, j)),
                  pl.BlockSpec((None, 1, tn), lambda l, j: (l, 0, j))],
        out_specs=pl.BlockSpec((None, 1, tn), lambda l, j: (l, 0, j)),
        compiler_params=_params(("parallel", "parallel")),
        name="ada_mod",
    )(c8, w_ada, b_ada.reshape(depth, 1, n))


def _norm_mod(x, g, shift, scale):
    ms = jnp.mean(x * x, axis=-1, keepdims=True)
    y = x * lax.rsqrt(ms + EPS) * g
    return y * (1.0 + scale) + shift


def _inproj_kernel(x_ref, g_ref, sh_ref, sc_ref, w_ref, wg_ref, wu_ref, o_ref, og_ref, ou_ref, h_scr):
    @pl.when(pl.program_id(1) == 0)
    def _():
        hb = _norm_mod(x_ref[...], g_ref[...], sh_ref[...], sc_ref[...]).astype(BF16)
        h_scr[...] = hb
        og_ref[...] = jnp.dot(hb, wg_ref[...], preferred_element_type=F32)
        ou_ref[...] = lax.dot_general(wu_ref[...], hb, (((1,), (1,)), ((), ())),
                                      preferred_element_type=F32).astype(ou_ref.dtype)

    o_ref[...] = jnp.dot(h_scr[...], w_ref[...], preferred_element_type=F32).astype(o_ref.dtype)


def in_proj(x, g, shift, scale, w_main, w_gate, w_s5_t):
    t, d = x.shape
    n = w_main.shape[1]
    nu = w_s5_t.shape[0]
    tm = min(1024, t)
    tn = 512
    vec = pl.BlockSpec((1, d), lambda i, j: (0, 0))
    return pl.pallas_call(
        _inproj_kernel,
        out_shape=(jax.ShapeDtypeStruct((t, n), BF16), jax.ShapeDtypeStruct((t, LANES), F32),
                   jax.ShapeDtypeStruct((nu, t), BF16)),
        grid=(t // tm, n // tn),
        in_specs=[pl.BlockSpec((tm, d), lambda i, j: (i, 0)), vec, vec, vec,
                  pl.BlockSpec((d, tn), lambda i, j: (0, j)),
                  pl.BlockSpec((d, LANES), lambda i, j: (0, 0)),
                  pl.BlockSpec((nu, d), lambda i, j: (0, 0))],
        out_specs=(pl.BlockSpec((tm, tn), lambda i, j: (i, j)),
                   pl.BlockSpec((tm, LANES), lambda i, j: (i, 0)),
                   pl.BlockSpec((nu, tm), lambda i, j: (0, i))),
        scratch_shapes=[pltpu.VMEM((tm, d), BF16)],
        compiler_params=_params(("parallel", "arbitrary")),
        name="in_proj",
    )(x, g, shift, scale, w_main, w_gate, w_s5_t)


def _sb_kernel(q_ref, k_ref, v_ref, o_ref, acc_ref, run_ref):
    blk = SB_BLOCK
    hd = SB_HEAD_DIM
    i = pl.program_id(1)
    scale = hd ** -0.5
    row = lax.broadcasted_iota(jnp.int32, (blk, blk), 0)
    col = lax.broadcasted_iota(jnp.int32, (blk, blk), 1)
    after = jnp.where(row > col, 1.0, 0.0).astype(BF16)
    causal = col < row
    acc_ref[...] = jnp.zeros(acc_ref.shape, F32)
    run_ref[...] = jnp.zeros(run_ref.shape, F32)

    def step(kb, diagonal):
        start = pl.multiple_of(kb * blk, blk)
        worst = None
        for h in range(SB_HEADS_PER_STEP):
            cs = slice(h * hd, (h + 1) * hd)
            z = lax.dot_general(q_ref[:, cs], k_ref[pl.ds(start, blk), cs], (((1,), (1,)), ((), ())),
                                preferred_element_type=F32) * scale
            softplus = jnp.maximum(z, 0.0) + jnp.log(1.0 + jnp.exp(-jnp.abs(z)))
            log_keep = -softplus
            log_beta = z - softplus
            if diagonal:
                log_keep = jnp.where(causal, log_keep, 0.0)
            hi = log_keep.astype(BF16)
            lo = (log_keep - hi.astype(F32)).astype(BF16)
            later = (jnp.dot(hi, after, preferred_element_type=F32)
                     + jnp.dot(lo, after, preferred_element_type=F32))
            run = run_ref[h]
            w = jnp.exp(log_beta + later + run)
            if diagonal:
                w = jnp.where(causal, w, 0.0)
            acc_ref[h] += jnp.dot(w.astype(BF16), v_ref[pl.ds(start, blk), cs], preferred_element_type=F32)
            run = run + jnp.sum(log_keep, axis=1, keepdims=True)
            run_ref[h] = run
            top = jnp.max(run)
            worst = top if worst is None else jnp.maximum(worst, top)
        return worst

    worst = step(i, True)

    def cond(carry):
        kb, worst = carry
        return jnp.logical_and(kb >= 0, worst > SB_LOG_CUTOFF)

    def body(carry):
        kb, _ = carry
        return kb - 1, step(kb, False)

    lax.while_loop(cond, body, (i - 1, worst))
    for h in range(SB_HEADS_PER_STEP):
        o_ref[:, h * hd:(h + 1) * hd] = acc_ref[h].astype(o_ref.dtype)


def sb_attention(proj):
    t = proj.shape[0]
    blk = SB_BLOCK
    wide = SB_HEADS_PER_STEP * SB_HEAD_DIM
    qb, kb, vb = COL_SB_Q // wide, COL_SB_K // wide, COL_SB_V // wide
    return pl.pallas_call(
        _sb_kernel,
        out_shape=jax.ShapeDtypeStruct((t, SB_WIDTH), BF16),
        grid=(SB_HEADS // SB_HEADS_PER_STEP, t // blk),
        in_specs=[pl.BlockSpec((blk, wide), lambda h, i: (i, qb + h)),
                  pl.BlockSpec((t, wide), lambda h, i: (0, kb + h)),
                  pl.BlockSpec((t, wide), lambda h, i: (0, vb + h))],
        out_specs=pl.BlockSpec((blk, wide), lambda h, i: (i, h)),
        scratch_shapes=[pltpu.VMEM((SB_HEADS_PER_STEP, blk, SB_HEAD_DIM), F32),
                        pltpu.VMEM((SB_HEADS_PER_STEP, blk, 1), F32)],
        compiler_params=_params(("parallel", "arbitrary")),
        name="sb_attention",
    )(proj, proj, proj)


def s5_weights(lam_re, lam_im, log_dt, b_re, b_im, c_re, c_im, d_skip):
    lc, g, p, hh = S5_CHUNK, S5_GROUPS, S5_STATE, S5_GROUP
    dt = jnp.exp(log_dt)[:, None]
    zr, zi = lam_re * dt, lam_im * dt
    e1 = jnp.exp(zr)
    lbr, lbi = e1 * jnp.cos(zi), e1 * jnp.sin(zi)
    den = lam_re * lam_re + lam_im * lam_im
    fr = ((lbr - 1.0) * lam_re + lbi * lam_im) / den
    fi = (lbi * lam_re - (lbr - 1.0) * lam_im) / den
    bbr = fr[:, :, None] * b_re - fi[:, :, None] * b_im
    bbi = fr[:, :, None] * b_im + fi[:, :, None] * b_re
    tau = jnp.arange(lc + 1, dtype=F32)[None, :, None]
    mag = jnp.exp(zr[:, None, :] * tau)
    pwr = mag * jnp.cos(zi[:, None, :] * tau)
    pwi = mag * jnp.sin(zi[:, None, :] * tau)
    cpr = c_re[:, None] * pwr[:, :, None, :] - c_im[:, None] * pwi[:, :, None, :]
    cpi = c_re[:, None] * pwi[:, :, None, :] + c_im[:, None] * pwr[:, :, None, :]
    kern = (jnp.einsum('gthp,gpk->gthk', cpr[:, :lc], bbr, precision=HIGHEST)
            - jnp.einsum('gthp,gpk->gthk', cpi[:, :lc], bbi, precision=HIGHEST))
    lane_pad = ((0, 0), (0, 0), (0, 0), (0, LANES - lc))
    k_rows = jnp.pad(kern.transpose(0, 3, 2, 1), lane_pad).reshape(g, hh, hh * LANES)
    rev_r, rev_i = pwr[:, lc - 1::-1][:, None, :lc], pwi[:, lc - 1::-1][:, None, :lc]
    bt_r, bt_i = bbr.transpose(0, 2, 1)[:, :, None, :], bbi.transpose(0, 2, 1)[:, :, None, :]
    w_state = jnp.concatenate([rev_r * bt_r - rev_i * bt_i, rev_r * bt_i + rev_i * bt_r],
                              axis=-1).reshape(g, hh * lc, 2 * p).astype(BF16)
    wor = jnp.pad(cpr[:, 1:].transpose(0, 3, 2, 1), lane_pad).reshape(g, p, hh * LANES)
    woi = jnp.pad(-cpi[:, 1:].transpose(0, 3, 2, 1), lane_pad).reshape(g, p, hh * LANES)
    w_out = jnp.concatenate([wor, woi], axis=1).astype(BF16)
    a_chunk = jnp.stack([pwr[:, lc], pwi[:, lc]], axis=1)
    d_row = jnp.broadcast_to(d_skip.reshape(g, 1, hh, 1), (g, 1, hh, LANES)).reshape(g, 1, hh * LANES)
    return k_rows, w_state, w_out, a_chunk, d_row


def _s5_kernel(u_ref, k_ref, ws_ref, wo_ref, a_ref, d_ref, o_ref, w_scr, sr_scr, si_scr, xr_scr, xi_scr):
    p = S5_STATE
    lc = S5_CHUNK
    hh = S5_GROUP
    nc = u_ref.shape[1]
    for hp in range(hh):
        rows = jnp.broadcast_to(k_ref[hp:hp + 1, :], (lc, hh * LANES))
        for h in range(hh):
            cs = slice(h * LANES, (h + 1) * LANES)
            w_scr[hp * lc:(hp + 1) * lc, cs] = pltpu.roll(rows[:, cs], 0, 1, stride=1, stride_axis=0).astype(BF16)

    uf = jnp.concatenate([u_ref[h].astype(F32) for h in range(hh)], axis=1)
    u = uf.astype(BF16)
    s = jnp.dot(u, ws_ref[...], preferred_element_type=F32)
    sr_scr[...] = s[:, :p]
    si_scr[...] = s[:, p:]
    ar = a_ref[0:1, :]
    ai = a_ref[1:2, :]

    def body(c, carry):
        xr, xi = carry
        xr_scr[pl.ds(c, 1), :] = xr
        xi_scr[pl.ds(c, 1), :] = xi
        sr = sr_scr[pl.ds(c, 1), :]
        si = si_scr[pl.ds(c, 1), :]
        return ar * xr - ai * xi + sr, ar * xi + ai * xr + si

    zero = jnp.zeros((1, p), F32)
    lax.fori_loop(0, nc, body, (zero, zero))
    y = jnp.dot(u, w_scr[...], preferred_element_type=F32)
    y = y + jnp.dot(xr_scr[...].astype(BF16), wo_ref[0:p, :], preferred_element_type=F32)
    y = y + jnp.dot(xi_scr[...].astype(BF16), wo_ref[p:2 * p, :], preferred_element_type=F32)
    for h in range(hh):
        cs = slice(h * LANES, h * LANES + lc)
        o_ref[h] = jax.nn.gelu(y[:, cs] + d_ref[:, cs] * u_ref[h].astype(F32)).astype(o_ref.dtype)


def s5_scan(u_t, k_rows, w_state, w_out, a_chunk, d_row):
    width, t = u_t.shape
    g, hh, lc, p = S5_GROUPS, S5_GROUP, S5_CHUNK, S5_STATE
    nc = t // lc
    grp = lambda n0, n1: pl.BlockSpec((None, n0, n1), lambda i: (i, 0, 0))
    chan = pl.BlockSpec((None, hh, nc, lc), lambda i: (i, 0, 0, 0))
    out = pl.pallas_call(
        _s5_kernel,
        out_shape=jax.ShapeDtypeStruct((g, hh, nc, lc), BF16),
        grid=(g,),
        in_specs=[chan, grp(hh, hh * LANES), grp(hh * lc, 2 * p), grp(2 * p, hh * LANES),
                  grp(2, p), grp(1, hh * LANES)],
        out_specs=chan,
        scratch_shapes=[pltpu.VMEM((hh * lc, hh * LANES), BF16)] + [pltpu.VMEM((nc, p), F32)] * 4,
        compiler_params=_params(("parallel",)),
        name="s5_scan",
    )(u_t.reshape(g, hh, nc, lc), k_rows, w_state, w_out, a_chunk, d_row)
    return out.reshape(width, t)


def _glu_kernel(y_ref, w_ref, o_ref):
    y = y_ref[...]
    a = jnp.dot(w_ref[...], y, preferred_element_type=F32)
    o_ref[...] = (y.astype(F32) * jax.nn.sigmoid(a)).T.astype(o_ref.dtype)


def s5_glu(y_t, w_glu_t):
    n, t = y_t.shape
    tn = min(1024, t)
    return pl.pallas_call(
        _glu_kernel,
        out_shape=jax.ShapeDtypeStruct((t, n), BF16),
        grid=(t // tn,),
        in_specs=[pl.BlockSpec((n, tn), lambda i: (0, i)),
                  pl.BlockSpec((n, n), lambda i: (0, 0))],
        out_specs=pl.BlockSpec((tn, n), lambda i: (i, 0)),
        compiler_params=_params(("parallel",)),
        name="s5_glu",
    )(y_t, w_glu_t)


ML_EXT = ML_HEAD_DIM + LANES


def _mlstm_kernel(q_ref, k_ref, v_ref, og_ref, g_ref, cw_ref, bias_ref, o_ref,
                  qbuf, kbuf, st_scr, m_scr):
    L = ML_CHUNK
    dh = ML_HEAD_DIM
    halo = SUBLANES
    c = pl.program_id(0)

    @pl.when(c == 0)
    def _():
        qbuf[0:halo, :] = jnp.zeros((halo, ML_WIDTH), F32)
        kbuf[0:halo, :] = jnp.zeros((halo, ML_WIDTH), F32)
        st_scr[...] = jnp.zeros(st_scr.shape, F32)
        m_scr[...] = jnp.zeros(m_scr.shape, F32)

    qbuf[halo:halo + L, :] = q_ref[...].astype(F32)
    kbuf[halo:halo + L, :] = k_ref[...].astype(F32)

    gp = g_ref[...] + bias_ref[...]
    lane = lax.broadcasted_iota(jnp.int32, gp.shape, 1)
    gl = jnp.where(lane < ML_HEADS, gp, _log_sigmoid(gp))
    r_i = lax.broadcasted_iota(jnp.int32, (L, L), 0)
    c_i = lax.broadcasted_iota(jnp.int32, (L, L), 1)
    tril = r_i >= c_i
    csum = jnp.dot(jnp.where(tril, 1.0, 0.0), gl, precision=HIGHEST,
                   preferred_element_type=F32)
    gl_t = gl.T
    csum_t = csum.T
    ones_col = (lax.broadcasted_iota(jnp.int32, (L, LANES), 1) == 0).astype(F32)

    for h in range(ML_HEADS):
        cs = slice(h * dh, (h + 1) * dh)

        def conv(buf, part):
            acc = None
            for j in range(ML_CONV):
                off = halo - (ML_CONV - 1) + j
                term = buf[off:off + L, cs] * cw_ref[j:j + 1, part * ML_WIDTH + h * dh:part * ML_WIDTH + (h + 1) * dh]
                acc = term if acc is None else acc + term
            return acc * jax.nn.sigmoid(acc)

        qh = conv(qbuf, 0)
        kh = conv(kbuf, 1) * (dh ** -0.5)
        vh = v_ref[:, cs]
        v_ext = jnp.concatenate([vh.astype(F32), ones_col], axis=1)

        li_col = gl[:, h:h + 1]
        b_col = csum[:, ML_HEADS + h:ML_HEADS + h + 1]
        li_row = gl_t[h:h + 1, :]
        b_row = csum_t[ML_HEADS + h:ML_HEADS + h + 1, :]
        b_tot = b_row[:, L - 1:L]
        m0 = m_scr[h:h + 1, 0:1]

        a_col = b_tot - b_col + li_col
        m_loc = jnp.max(a_col, axis=0, keepdims=True)
        wa = jnp.exp(a_col - m_loc)

        qb = qh.astype(BF16)
        qk = lax.dot_general(qb, kh.astype(BF16), (((1,), (1,)), ((), ())),
                             preferred_element_type=F32)
        log_d = jnp.where(tril, b_col - b_row + li_row, -jnp.inf)
        inter_log = b_col + m0
        m_t = jnp.maximum(inter_log, jnp.max(log_d, axis=1, keepdims=True))
        s_mat = jnp.exp(log_d - m_t) * qk
        w_inter = jnp.exp(inter_log - m_t)
        state = st_scr[h]
        nd = (jnp.dot(s_mat.astype(BF16), v_ext.astype(BF16), preferred_element_type=F32)
              + w_inter * jnp.dot(qb, state.astype(BF16), preferred_element_type=F32))
        num = nd[:, :dh]
        den = nd[:, dh:dh + 1]
        hval = num / jnp.maximum(jnp.abs(den), jnp.exp(-m_t))
        gate = jax.nn.sigmoid(og_ref[:, cs].astype(F32))
        o_ref[:, cs] = (hval * gate).astype(o_ref.dtype)

        m_new = jnp.maximum(b_tot + m0, m_loc)
        s_old = jnp.exp(b_tot + m0 - m_new)
        s_new = jnp.exp(m_loc - m_new)
        upd = jnp.dot(kh.T.astype(BF16), (wa * v_ext).astype(BF16), preferred_element_type=F32)
        st_scr[h] = s_old * state + s_new * upd
        m_scr[h:h + 1, :] = jnp.broadcast_to(m_new, (1, LANES))

    qbuf[0:halo, :] = qbuf[L:L + halo, :]
    kbuf[0:halo, :] = kbuf[L:L + halo, :]


def mlstm(proj, gates, conv_w, i_bias, f_bias):
    t = proj.shape[0]
    L = ML_CHUNK
    wb = ML_WIDTH
    bias = jnp.zeros((1, LANES), F32).at[0, :ML_HEADS].set(i_bias).at[0, ML_HEADS:2 * ML_HEADS].set(f_bias)
    col = lambda off: pl.BlockSpec((L, wb), lambda c, off=off: (c, off // wb))
    return pl.pallas_call(
        _mlstm_kernel,
        out_shape=jax.ShapeDtypeStruct((t, wb), BF16),
        grid=(t // L,),
        in_specs=[col(COL_ML_Q), col(COL_ML_K), col(COL_ML_V), col(COL_ML_O),
                  pl.BlockSpec((L, LANES), lambda c: (c, 0)),
                  pl.BlockSpec((ML_CONV, 2 * wb), lambda c: (0, 0)),
                  pl.BlockSpec((1, LANES), lambda c: (0, 0))],
        out_specs=pl.BlockSpec((L, wb), lambda c: (c, 0)),
        scratch_shapes=[pltpu.VMEM((L + SUBLANES, wb), F32), pltpu.VMEM((L + SUBLANES, wb), F32),
                        pltpu.VMEM((ML_HEADS, ML_HEAD_DIM, ML_EXT), F32),
                        pltpu.VMEM((SUBLANES, LANES), F32)],
        compiler_params=_params(("arbitrary",)),
        name="mlstm",
    )(proj, proj, proj, proj, gates, conv_w, bias)


def _merge_kernel(ysb_ref, ys5_ref, yml_ref, psb_ref, ps5_ref, pml_ref,
                  gsb_ref, gs5_ref, gml_ref, o_ref):
    def branch(y_ref, p_ref, g_ref):
        return (jax.nn.sigmoid(g_ref[...].astype(F32))
                * jnp.dot(y_ref[...], p_ref[...], preferred_element_type=F32))

    o_ref[...] = (branch(ysb_ref, psb_ref, gsb_ref) + branch(ys5_ref, ps5_ref, gs5_ref)
                  + branch(yml_ref, pml_ref, gml_ref)).astype(o_ref.dtype)


def merge(y_sb, y_s5, y_ml, p_sb, p_s5, p_ml, proj):
    t, k = y_sb.shape
    n = p_sb.shape[1]
    tm, tn = min(1024, t), 512
    ysp = pl.BlockSpec((tm, k), lambda i, j: (i, 0))
    psp = pl.BlockSpec((k, tn), lambda i, j: (0, j))
    gsp = lambda off: pl.BlockSpec((tm, tn), lambda i, j, off=off: (i, off // tn + j))
    return pl.pallas_call(
        _merge_kernel,
        out_shape=jax.ShapeDtypeStruct((t, n), BF16),
        grid=(t // tm, n // tn),
        in_specs=[ysp, ysp, ysp, psp, psp, psp, gsp(COL_G_SB), gsp(COL_G_S5), gsp(COL_G_ML)],
        out_specs=pl.BlockSpec((tm, tn), lambda i, j: (i, j)),
        compiler_params=_params(("parallel", "parallel")),
        name="merge",
    )(y_sb, y_s5, y_ml, p_sb, p_s5, p_ml, proj, proj, proj)


def _route(logits):
    lane = lax.broadcasted_iota(jnp.int32, logits.shape, 1)
    neg = -jnp.inf
    big = jnp.int32(LANES)
    is_g = jnp.logical_and(lane >= N_EXPERTS, lane < N_EXPERTS + N_GROUPS)
    gl = jnp.where(is_g, logits, neg)
    gmax = jnp.max(gl, axis=1, keepdims=True)
    g_idx = jnp.min(jnp.where(gl == gmax, lane, big), axis=1, keepdims=True) - N_EXPERTS
    g_w = 1.0 / jnp.sum(jnp.where(is_g, jnp.exp(gl - gmax), 0.0), axis=1, keepdims=True)
    in_grp = jnp.logical_and(lane >= g_idx * EXPERTS_PER_GROUP, lane < (g_idx + 1) * EXPERTS_PER_GROUP)
    e1 = jnp.where(in_grp, logits, neg)
    v1 = jnp.max(e1, axis=1, keepdims=True)
    i1 = jnp.min(jnp.where(e1 == v1, lane, big), axis=1, keepdims=True)
    e2 = jnp.where(lane == i1, neg, e1)
    v2 = jnp.max(e2, axis=1, keepdims=True)
    i2 = jnp.min(jnp.where(e2 == v2, lane, big), axis=1, keepdims=True)
    ex = jnp.exp(v2 - v1)
    return i1, i2, g_w / (1.0 + ex), g_w * ex / (1.0 + ex)


META_E1, META_E2, META_W1, META_W2, META_POS1, META_POS2 = range(6)


def _outproj_kernel(m_ref, w_ref, x_ref, g1_ref, ng_ref, sh_ref, sc_ref, wr_ref, br_ref,
                    xo_ref, h_ref, meta_ref, cnt_ref, carry):
    @pl.when(pl.program_id(0) == 0)
    def _():
        carry[...] = jnp.zeros(carry.shape, F32)

    y = jnp.dot(m_ref[...], w_ref[...], preferred_element_type=F32)
    x = x_ref[...] + g1_ref[...] * y
    xo_ref[...] = x
    h = _norm_mod(x, ng_ref[...], sh_ref[...], sc_ref[...])
    h_ref[...] = h
    logits = jnp.dot(h.astype(BF16), wr_ref[...], preferred_element_type=F32) + br_ref[...]
    i1, i2, w1, w2 = _route(logits)

    tm = logits.shape[0]
    lane = lax.broadcasted_iota(jnp.int32, logits.shape, 1)
    picked = jnp.logical_or(lane == i1, lane == i2)
    r_i = lax.broadcasted_iota(jnp.int32, (tm, tm), 0)
    c_i = lax.broadcasted_iota(jnp.int32, (tm, tm), 1)
    earlier = jnp.where(r_i > c_i, 1.0, 0.0).astype(BF16)
    before = jnp.dot(earlier, jnp.where(picked, 1.0, 0.0).astype(BF16), preferred_element_type=F32)
    before = before + carry[0:1, :]
    pos1 = jnp.sum(jnp.where(lane == i1, before, 0.0), axis=1, keepdims=True)
    pos2 = jnp.sum(jnp.where(lane == i2, before, 0.0), axis=1, keepdims=True)
    total = carry[0:1, :] + jnp.sum(jnp.where(picked, 1.0, 0.0), axis=0, keepdims=True)
    carry[...] = jnp.broadcast_to(total, carry.shape)
    cnt_ref[...] = jnp.broadcast_to(total, cnt_ref.shape)

    meta = jnp.zeros(logits.shape, F32)
    for k, val in ((META_E1, i1.astype(F32)), (META_E2, i2.astype(F32)), (META_W1, w1), (META_W2, w2),
                   (META_POS1, pos1), (META_POS2, pos2)):
        meta = jnp.where(lane == k, val, meta)
    meta_ref[...] = meta


def out_proj(merged, w_out, x, gate1, norm_g, shift, scale, w_route, b_route):
    t, d = x.shape
    tm = min(256, t)
    row = pl.BlockSpec((tm, d), lambda i: (i, 0))
    vec = pl.BlockSpec((1, d), lambda i: (0, 0))
    return pl.pallas_call(
        _outproj_kernel,
        out_shape=(jax.ShapeDtypeStruct((t, d), F32), jax.ShapeDtypeStruct((t, d), F32),
                   jax.ShapeDtypeStruct((t, LANES), F32), jax.ShapeDtypeStruct((SUBLANES, LANES), F32)),
        grid=(t // tm,),
        in_specs=[row, pl.BlockSpec((d, d), lambda i: (0, 0)), row, vec, vec, vec, vec,
                  pl.BlockSpec((d, LANES), lambda i: (0, 0)), pl.BlockSpec((1, LANES), lambda i: (0, 0))],
        out_specs=(row, row, pl.BlockSpec((tm, LANES), lambda i: (i, 0)),
                   pl.BlockSpec((SUBLANES, LANES), lambda i: (0, 0))),
        scratch_shapes=[pltpu.VMEM((SUBLANES, LANES), F32)],
        compiler_params=_params(("arbitrary",)),
        name="out_proj",
    )(merged, w_out, x, gate1, norm_g, shift, scale, w_route, b_route)


def moe_plan(meta, counts, n_tiles):
    tile = MOE_TILE
    as_int = lambda k: meta[:, k].astype(jnp.int32)
    cnt = counts[0, :N_EXPERTS].astype(jnp.int32)
    padded = ((cnt + tile - 1) // tile) * tile
    seg_end = jnp.cumsum(padded)
    seg_off = seg_end - padded
    n_used = (seg_end[-1] // tile).reshape(1)
    starts = jnp.arange(n_tiles, dtype=jnp.int32) * tile
    tile_expert = jnp.sum((starts[:, None] >= seg_end[None, :]).astype(jnp.int32), axis=1)
    tile_expert = jnp.minimum(tile_expert, N_EXPERTS - 1)
    picks = (as_int(META_E1), as_int(META_E2), as_int(META_POS1), as_int(META_POS2), seg_off)
    return picks, tile_expert, n_used


def _row_copy(src, src_row, dst, dst_row, sem):
    return pltpu.make_async_copy(src.at[pl.ds(src_row, 1)], dst.at[pl.ds(dst_row, 1)], sem)


def _experts_kernel(e1_ref, e2_ref, p1_ref, p2_ref, off_ref, te_ref, nu_ref,
                    h_ref, w1_ref, w3_ref, w2_ref, y_ref, token_of, buf, sems, w1b, w3b, w2b):
    j = pl.program_id(0)
    tile = MOE_TILE
    n_used = nu_ref[0]

    @pl.when(j == 0)
    def _():
        def clear(s, carry):
            token_of[s] = 0
            return carry

        def fill(t, carry):
            token_of[off_ref[e1_ref[t]] + p1_ref[t]] = t
            token_of[off_ref[e2_ref[t]] + p2_ref[t]] = t
            return carry

        lax.fori_loop(0, token_of.shape[0], clear, 0, unroll=8)
        lax.fori_loop(0, e1_ref.shape[0], fill, 0, unroll=4)

        def first(r, carry):
            _row_copy(h_ref, token_of[r], buf.at[0], r, sems.at[0]).start()
            return carry

        lax.fori_loop(0, tile, first, 0, unroll=4)

    def tile_done(slot):
        return pltpu.make_async_copy(h_ref.at[pl.ds(0, tile)], buf.at[slot], sems.at[slot])

    def run_tile(slot):
        base = jnp.minimum(j + 1, n_used - 1) * tile
        for r in range(tile):
            _row_copy(h_ref, token_of[base + r], buf.at[1 - slot], r, sems.at[1 - slot]).start()
        tile_done(slot).wait()
        x = buf[slot].astype(BF16)
        a = jnp.dot(x, w1b[...], preferred_element_type=F32)
        b = jnp.dot(x, w3b[...], preferred_element_type=F32)
        hid = (a * jax.nn.sigmoid(a) * b).astype(BF16)
        y_ref[...] = jnp.dot(hid, w2b[...], preferred_element_type=F32)

        @pl.when(j == n_used - 1)
        def _():
            tile_done(1 - slot).wait()

    @pl.when(j < n_used)
    def _():
        @pl.when(jnp.logical_or(j == 0, te_ref[j] != te_ref[jnp.maximum(j - 1, 0)]))
        def _():
            w1b[...] = w1_ref[...].astype(BF16)
            w3b[...] = w3_ref[...].astype(BF16)
            w2b[...] = w2_ref[...].astype(BF16)

        for slot in range(2):
            pl.when(j % 2 == slot)(functools.partial(run_tile, slot))

    @pl.when(j >= n_used)
    def _():
        y_ref[...] = jnp.zeros(y_ref.shape, y_ref.dtype)


def moe_experts(h, picks, tile_expert, n_used, w1, w3, w2, n_slots):
    t, d = h.shape
    ne, _, ff = w1.shape
    tile = MOE_TILE
    n_pref = len(picks) + 2

    def wspec(r, c):
        def index(j, *pref):
            te, nu = pref[-2], pref[-1]
            return (te[jnp.minimum(j, nu[0] - 1)], 0, 0)
        return pl.BlockSpec((None, r, c), index)

    return pl.pallas_call(
        _experts_kernel,
        out_shape=jax.ShapeDtypeStruct((n_slots, d), F32),
        grid_spec=pltpu.PrefetchScalarGridSpec(
            num_scalar_prefetch=n_pref,
            grid=(n_slots // tile,),
            in_specs=[pl.BlockSpec(memory_space=pl.ANY), wspec(d, ff), wspec(d, ff), wspec(ff, d)],
            out_specs=pl.BlockSpec((tile, d), lambda j, *pref: (j, 0)),
            scratch_shapes=[pltpu.SMEM((n_slots,), jnp.int32), pltpu.VMEM((2, tile, d), F32),
                            pltpu.SemaphoreType.DMA((2,)),
                            pltpu.VMEM((d, ff), BF16), pltpu.VMEM((d, ff), BF16), pltpu.VMEM((ff, d), BF16)]),
        compiler_params=_params(("arbitrary",)),
        name="moe_experts",
    )(*picks, tile_expert, n_used, h, w1, w3, w2)


def _combine_kernel(e1_ref, e2_ref, p1_ref, p2_ref, off_ref, y_ref, x_ref, meta_ref, g2_ref, fg_ref,
                    o_ref, buf, sems, *, final_norm):
    i = pl.program_id(0)
    tm = x_ref.shape[0]

    def gather(tile_idx, slot):
        base = tile_idx * tm

        def issue(r, carry):
            t = base + r
            _row_copy(y_ref, off_ref[e1_ref[t]] + p1_ref[t], buf.at[slot, 0], r, sems.at[slot]).start()
            _row_copy(y_ref, off_ref[e2_ref[t]] + p2_ref[t], buf.at[slot, 1], r, sems.at[slot]).start()
            return carry

        lax.fori_loop(0, tm, issue, 0, unroll=4)

    @pl.when(i == 0)
    def _():
        gather(0, 0)

    @pl.when(i + 1 < pl.num_programs(0))
    def _():
        gather(i + 1, (i + 1) % 2)

    slot = i % 2
    for k in range(2):
        pltpu.make_async_copy(y_ref.at[pl.ds(0, tm)], buf.at[slot, k], sems.at[slot]).wait()

    meta = meta_ref[...]
    mix = meta[:, META_W1:META_W1 + 1] * buf[slot, 0] + meta[:, META_W2:META_W2 + 1] * buf[slot, 1]
    x = x_ref[...] + g2_ref[...] * mix
    if final_norm:
        ms = jnp.mean(x * x, axis=-1, keepdims=True)
        x = x * lax.rsqrt(ms + EPS) * fg_ref[...]
    o_ref[...] = x


def moe_combine(ys, picks, x, meta, gate2, final_g, final_norm):
    t, d = x.shape
    tm = min(256, t)
    row = pl.BlockSpec((tm, d), lambda i, *pref: (i, 0))
    vec = pl.BlockSpec((1, d), lambda i, *pref: (0, 0))
    return pl.pallas_call(
        functools.partial(_combine_kernel, final_norm=final_norm),
        out_shape=jax.ShapeDtypeStruct((t, d), F32),
        grid_spec=pltpu.PrefetchScalarGridSpec(
            num_scalar_prefetch=len(picks),
            grid=(t // tm,),
            in_specs=[pl.BlockSpec(memory_space=pl.ANY), row,
                      pl.BlockSpec((tm, LANES), lambda i, *pref: (i, 0)), vec, vec],
            out_specs=row,
            scratch_shapes=[pltpu.VMEM((2, 2, tm, d), F32), pltpu.SemaphoreType.DMA((2,))]),
        compiler_params=_params(("arbitrary",)),
        name="moe_combine",
    )(*picks, ys, x, meta, gate2, final_g)


def kernel(x, c, norm_mix_g, norm_moe_g, final_g, w_ada, b_ada, w_in, ml_conv, ml_i_bias, ml_f_bias,
           s5_lam_re, s5_lam_im, s5_log_dt, s5_b_re, s5_b_im, s5_c_re, s5_c_im, s5_d, s5_w_glu,
           p_sb, p_s5, p_ml, w_out, moe_w_group, moe_b_group, moe_w_router, moe_b_router,
           moe_w1, moe_w3, moe_w2):
    bsz, seq, d = x.shape
    assert bsz == 1 and d == D_MODEL
    assert seq % max(ML_CHUNK, SB_BLOCK, S5_CHUNK * SUBLANES) == 0
    depth = w_in.shape[0]
    xt = x.reshape(seq, d)
    mod = ada_mod(c, w_ada, b_ada)
    for l in range(depth):
        shift1, scale1, gate1, shift2, scale2, gate2 = [mod[l, :, i * d:(i + 1) * d] for i in range(6)]
        w_l = w_in[l]
        w_main = jnp.concatenate([w_l[:, :WIN_S5_U], w_l[:, WIN_ML_Q:WIN_GATES], w_l[:, WIN_G_SB:]],
                                 axis=1).astype(BF16)
        w_gate = jnp.pad(w_l[:, WIN_GATES:WIN_G_SB], ((0, 0), (0, LANES - N_GATE_COLS))).astype(BF16)
        w_s5_t = w_l[:, WIN_S5_U:WIN_ML_Q].T.astype(BF16)
        proj, gates, u_t = in_proj(xt, norm_mix_g[l].reshape(1, d), shift1, scale1, w_main, w_gate, w_s5_t)

        y_sb = sb_attention(proj)

        y_t = s5_scan(u_t, *s5_weights(s5_lam_re[l], s5_lam_im[l], s5_log_dt[l], s5_b_re[l], s5_b_im[l],
                                       s5_c_re[l], s5_c_im[l], s5_d[l]))
        y_s5 = s5_glu(y_t, s5_w_glu[l].T.astype(BF16))

        y_ml = mlstm(proj, gates, ml_conv[l], ml_i_bias[l], ml_f_bias[l])

        merged = merge(y_sb, y_s5, y_ml, p_sb[l].astype(BF16), p_s5[l].astype(BF16), p_ml[l].astype(BF16), proj)

        w_route = jnp.pad(jnp.concatenate([moe_w_router[l], moe_w_group[l]], axis=1),
                          ((0, 0), (0, LANES - N_EXPERTS - N_GROUPS))).astype(BF16)
        b_route = jnp.pad(jnp.concatenate([moe_b_router[l], moe_b_group[l]]),
                          (0, LANES - N_EXPERTS - N_GROUPS)).reshape(1, LANES)
        xt, h2, meta, counts = out_proj(merged, w_out[l].astype(BF16), xt, gate1, norm_moe_g[l].reshape(1, d),
                                        shift2, scale2, w_route, b_route)
        n_slots = 2 * seq + N_EXPERTS * MOE_TILE
        picks, tile_expert, n_used = moe_plan(meta, counts, n_slots // MOE_TILE)
        ys = moe_experts(h2, picks, tile_expert, n_used, moe_w1[l], moe_w3[l], moe_w2[l], n_slots)
        xt = moe_combine(ys, picks, xt, meta, gate2, final_g.reshape(1, d), final_norm=(l == depth - 1))
    return xt.reshape(bsz, seq, d)
```

```python
import functools
import math

import jax
import jax.numpy as jnp
from jax import lax
from jax.experimental import pallas as pl
from jax.experimental.pallas import tpu as pltpu

F32 = jnp.float32
BF16 = jnp.bfloat16
HIGHEST = lax.Precision.HIGHEST

D_MODEL = 2048
DEPTH = 2
SB_HEADS = 8
SB_HEAD_DIM = 128
SB_WIDTH = SB_HEADS * SB_HEAD_DIM
S5_WIDTH = 1024
S5_GROUP = 16
S5_GROUPS = S5_WIDTH // S5_GROUP
S5_STATE = 64
ML_HEADS = 4
ML_HEAD_DIM = 256
ML_WIDTH = ML_HEADS * ML_HEAD_DIM
ML_CONV = 4
N_GROUPS = 4
EXPERTS_PER_GROUP = 8
N_EXPERTS = N_GROUPS * EXPERTS_PER_GROUP
EXPERT_FF = 256
EPS = 1e-6

LANES = 128
SUBLANES = 8
VMEM_LIMIT = 52 * 1024 * 1024

WIN_S5_U = 3072
WIN_ML_Q = 4096
WIN_GATES = 8192
N_GATE_COLS = 2 * ML_HEADS
WIN_G_SB = WIN_GATES + N_GATE_COLS
COL_SB_Q, COL_SB_K, COL_SB_V = 0, 1024, 2048
COL_ML_Q, COL_ML_K, COL_ML_V, COL_ML_O = 3072, 4096, 5120, 6144
COL_G_SB, COL_G_S5, COL_G_ML = 7168, 9216, 11264

S5_CHUNK = 64
ML_CHUNK = 256
SB_BLOCK = 256
SB_HEADS_PER_STEP = 2
MOE_TILE = 256
SB_LOG_CUTOFF = -105.0


def _params(sem, vmem=VMEM_LIMIT):
    return pltpu.CompilerParams(dimension_semantics=sem, vmem_limit_bytes=vmem)


def _log_sigmoid(z):
    return jnp.minimum(z, 0.0) - jnp.log1p(jnp.exp(-jnp.abs(z)))


def _ada_kernel(c_ref, w_ref, b_ref, o_ref):
    c = c_ref[...]
    ca = c * jax.nn.sigmoid(c)
    r = jnp.dot(ca, w_ref[...], precision=HIGHEST, preferred_element_type=F32)
    o_ref[...] = r[0:1, :] + b_ref[...]


def ada_mod(c, w_ada, b_ada):
    depth, d, n = w_ada.shape
    tn = 1024
    c8 = jnp.broadcast_to(c.reshape(1, d), (SUBLANES, d))
    return pl.pallas_call(
        _ada_kernel,
        out_shape=jax.ShapeDtypeStruct((depth, 1, n), F32),
        grid=(depth, n // tn),
        in_specs=[pl.BlockSpec((SUBLANES, d), lambda l, j: (0, 0)),
                  pl.BlockSpec((None, d, tn), lambda l, j: (l, 0, j)),
                  pl.BlockSpec((None, 1, tn), lambda l, j: (l, 0, j))],
        out_specs=pl.BlockSpec((None, 1, tn), lambda l, j: (l, 0, j)),
        compiler_params=_params(("parallel", "parallel")),
        name="ada_mod",
    )(c8, w_ada, b_ada.reshape(depth, 1, n))


def _norm_mod(x, g, shift, scale):
    ms = jnp.mean(x * x, axis=-1, keepdims=True)
    y = x * lax.rsqrt(ms + EPS) * g
    return y * (1.0 + scale) + shift


def _inproj_kernel(x_ref, g_ref, sh_ref, sc_ref, w_ref, wg_ref, wu_ref, o_ref, og_ref, ou_ref, h_scr):
    @pl.when(pl.program_id(1) == 0)
    def _():
        hb = _norm_mod(x_ref[...], g_ref[...], sh_ref[...], sc_ref[...]).astype(BF16)
        h_scr[...] = hb
        og_ref[...] = jnp.dot(hb, wg_ref[...], preferred_element_type=F32)
        ou_ref[...] = lax.dot_general(wu_ref[...], hb, (((1,), (1,)), ((), ())),
                                      preferred_element_type=F32).astype(ou_ref.dtype)

    o_ref[...] = jnp.dot(h_scr[...], w_ref[...], preferred_element_type=F32).astype(o_ref.dtype)


def in_proj(x, g, shift, scale, w_main, w_gate, w_s5_t):
    t, d = x.shape
    n = w_main.shape[1]
    nu = w_s5_t.shape[0]
    tm = min(1024, t)
    tn = 512
    vec = pl.BlockSpec((1, d), lambda i, j: (0, 0))
    return pl.pallas_call(
        _inproj_kernel,
        out_shape=(jax.ShapeDtypeStruct((t, n), BF16), jax.ShapeDtypeStruct((t, LANES), F32),
                   jax.ShapeDtypeStruct((nu, t), BF16)),
        grid=(t // tm, n // tn),
        in_specs=[pl.BlockSpec((tm, d), lambda i, j: (i, 0)), vec, vec, vec,
                  pl.BlockSpec((d, tn), lambda i, j: (0, j)),
                  pl.BlockSpec((d, LANES), lambda i, j: (0, 0)),
                  pl.BlockSpec((nu, d), lambda i, j: (0, 0))],
        out_specs=(pl.BlockSpec((tm, tn), lambda i, j: (i, j)),
                   pl.BlockSpec((tm, LANES), lambda i, j: (i, 0)),
                   pl.BlockSpec((nu, tm), lambda i, j: (0, i))),
        scratch_shapes=[pltpu.VMEM((tm, d), BF16)],
        compiler_params=_params(("parallel", "arbitrary")),
        name="in_proj",
    )(x, g, shift, scale, w_main, w_gate, w_s5_t)


def _sb_kernel(q_ref, k_ref, v_ref, o_ref, acc_ref, run_ref):
    blk = SB_BLOCK
    hd = SB_HEAD_DIM
    i = pl.program_id(1)
    scale = hd ** -0.5
    row = lax.broadcasted_iota(jnp.int32, (blk, blk), 0)
    col = lax.broadcasted_iota(jnp.int32, (blk, blk), 1)
    after = jnp.where(row > col, 1.0, 0.0).astype(BF16)
    causal = col < row
    acc_ref[...] = jnp.zeros(acc_ref.shape, F32)
    run_ref[...] = jnp.zeros(run_ref.shape, F32)

    def step(kb, diagonal):
        start = pl.multiple_of(kb * blk, blk)
        worst = None
        for h in range(SB_HEADS_PER_STEP):
            cs = slice(h * hd, (h + 1) * hd)
            z = lax.dot_general(q_ref[:, cs], k_ref[pl.ds(start, blk), cs], (((1,), (1,)), ((), ())),
                                preferred_element_type=F32) * scale
            softplus = jnp.maximum(z, 0.0) + jnp.log(1.0 + jnp.exp(-jnp.abs(z)))
            log_keep = -softplus
            log_beta = z - softplus
            if diagonal:
                log_keep = jnp.where(causal, log_keep, 0.0)
            hi = log_keep.astype(BF16)
            lo = (log_keep - hi.astype(F32)).astype(BF16)
            later = (jnp.dot(hi, after, preferred_element_type=F32)
                     + jnp.dot(lo, after, preferred_element_type=F32))
            run = run_ref[h]
            w = jnp.exp(log_beta + later + run)
            if diagonal:
                w = jnp.where(causal, w, 0.0)
            acc_ref[h] += jnp.dot(w.astype(BF16), v_ref[pl.ds(start, blk), cs], preferred_element_type=F32)
            run = run + jnp.sum(log_keep, axis=1, keepdims=True)
            run_ref[h] = run
            top = jnp.max(run)
            worst = top if worst is None else jnp.maximum(worst, top)
        return worst

    worst = step(i, True)

    def cond(carry):
        kb, worst = carry
        return jnp.logical_and(kb >= 0, worst > SB_LOG_CUTOFF)

    def body(carry):
        kb, _ = carry
        return kb - 1, step(kb, False)

    lax.while_loop(cond, body, (i - 1, worst))
    for h in range(SB_HEADS_PER_STEP):
        o_ref[:, h * hd:(h + 1) * hd] = acc_ref[h].astype(o_ref.dtype)


def sb_attention(proj):
    t = proj.shape[0]
    blk = SB_BLOCK
    wide = SB_HEADS_PER_STEP * SB_HEAD_DIM
    qb, kb, vb = COL_SB_Q // wide, COL_SB_K // wide, COL_SB_V // wide
    return pl.pallas_call(
        _sb_kernel,
        out_shape=jax.ShapeDtypeStruct((t, SB_WIDTH), BF16),
        grid=(SB_HEADS // SB_HEADS_PER_STEP, t // blk),
        in_specs=[pl.BlockSpec((blk, wide), lambda h, i: (i, qb + h)),
                  pl.BlockSpec((t, wide), lambda h, i: (0, kb + h)),
                  pl.BlockSpec((t, wide), lambda h, i: (0, vb + h))],
        out_specs=pl.BlockSpec((blk, wide), lambda h, i: (i, h)),
        scratch_shapes=[pltpu.VMEM((SB_HEADS_PER_STEP, blk, SB_HEAD_DIM), F32),
                        pltpu.VMEM((SB_HEADS_PER_STEP, blk, 1), F32)],
        compiler_params=_params(("parallel", "arbitrary")),
        name="sb_attention",
    )(proj, proj, proj)


def s5_weights(lam_re, lam_im, log_dt, b_re, b_im, c_re, c_im, d_skip):
    lc, g, p, hh = S5_CHUNK, S5_GROUPS, S5_STATE, S5_GROUP
    dt = jnp.exp(log_dt)[:, None]
    zr, zi = lam_re * dt, lam_im * dt
    e1 = jnp.exp(zr)
    lbr, lbi = e1 * jnp.cos(zi), e1 * jnp.sin(zi)
    den = lam_re * lam_re + lam_im * lam_im
    fr = ((lbr - 1.0) * lam_re + lbi * lam_im) / den
    fi = (lbi * lam_re - (lbr - 1.0) * lam_im) / den
    bbr = fr[:, :, None] * b_re - fi[:, :, None] * b_im
    bbi = fr[:, :, None] * b_im + fi[:, :, None] * b_re
    tau = jnp.arange(LANES, dtype=F32)
    used = tau < lc

    def c_times_power(shift):
        e = (tau + shift) * used
        mag = jnp.exp(zr[:, :, None, None] * e)
        pr = mag * jnp.cos(zi[:, :, None, None] * e)
        pi = mag * jnp.sin(zi[:, :, None, None] * e)
        cr = c_re.transpose(0, 2, 1)[:, :, :, None]
        ci = c_im.transpose(0, 2, 1)[:, :, :, None]
        return jnp.where(used, cr * pr - ci * pi, 0.0), jnp.where(used, cr * pi + ci * pr, 0.0)

    cp0r, cp0i = c_times_power(0.0)
    k_rows = (jnp.einsum('gpk,gpn->gkn', bbr, cp0r.reshape(g, p, hh * LANES), precision=HIGHEST)
              - jnp.einsum('gpk,gpn->gkn', bbi, cp0i.reshape(g, p, hh * LANES), precision=HIGHEST))
    cp1r, cp1i = c_times_power(1.0)
    w_out = jnp.concatenate([cp1r, -cp1i], axis=1).reshape(g, 2 * p, hh * LANES).astype(BF16)
    rev = (lc - 1.0) - jnp.arange(lc, dtype=F32)[None, :, None]
    rmag = jnp.exp(zr[:, None, :] * rev)
    rev_r, rev_i = rmag * jnp.cos(zi[:, None, :] * rev), rmag * jnp.sin(zi[:, None, :] * rev)
    bt_r, bt_i = bbr.transpose(0, 2, 1), bbi.transpose(0, 2, 1)
    w_state = (jnp.concatenate([rev_r, rev_r], -1)[:, None] * jnp.concatenate([bt_r, bt_i], -1)[:, :, None]
               + jnp.concatenate([-rev_i, rev_i], -1)[:, None] * jnp.concatenate([bt_i, bt_r], -1)[:, :, None]
               ).reshape(g, hh * lc, 2 * p).astype(BF16)
    full = jnp.exp(zr * lc)
    a_chunk = jnp.stack([full * jnp.cos(zi * lc), full * jnp.sin(zi * lc)], axis=1)
    d_row = jnp.broadcast_to(d_skip.reshape(g, 1, hh, 1), (g, 1, hh, LANES)).reshape(g, 1, hh * LANES)
    return k_rows, w_state, w_out, a_chunk, d_row


def _s5_kernel(u_ref, k_ref, ws_ref, wo_ref, a_ref, d_ref, o_ref, w_scr, sr_scr, si_scr, xr_scr, xi_scr):
    p = S5_STATE
    lc = S5_CHUNK
    hh = S5_GROUP
    nc = u_ref.shape[1]
    for hp in range(hh):
        rows = jnp.broadcast_to(k_ref[hp:hp + 1, :], (lc, hh * LANES))
        for h in range(hh):
            cs = slice(h * LANES, (h + 1) * LANES)
            w_scr[hp * lc:(hp + 1) * lc, cs] = pltpu.roll(rows[:, cs], 0, 1, stride=1, stride_axis=0).astype(BF16)

    uf = jnp.concatenate([u_ref[h].astype(F32) for h in range(hh)], axis=1)
    u = uf.astype(BF16)
    s = jnp.dot(u, ws_ref[...], preferred_element_type=F32)
    sr_scr[...] = s[:, :p]
    si_scr[...] = s[:, p:]
    ar = a_ref[0:1, :]
    ai = a_ref[1:2, :]

    def body(c, carry):
        xr, xi = carry
        xr_scr[pl.ds(c, 1), :] = xr
        xi_scr[pl.ds(c, 1), :] = xi
        sr = sr_scr[pl.ds(c, 1), :]
        si = si_scr[pl.ds(c, 1), :]
        return ar * xr - ai * xi + sr, ar * xi + ai * xr + si

    zero = jnp.zeros((1, p), F32)
    lax.fori_loop(0, nc, body, (zero, zero))
    y = jnp.dot(u, w_scr[...], preferred_element_type=F32)
    y = y + jnp.dot(xr_scr[...].astype(BF16), wo_ref[0:p, :], preferred_element_type=F32)
    y = y + jnp.dot(xi_scr[...].astype(BF16), wo_ref[p:2 * p, :], preferred_element_type=F32)
    for h in range(hh):
        cs = slice(h * LANES, h * LANES + lc)
        o_ref[h] = jax.nn.gelu(y[:, cs] + d_ref[:, cs] * u_ref[h].astype(F32)).astype(o_ref.dtype)


def s5_scan(u_t, k_rows, w_state, w_out, a_chunk, d_row):
    width, t = u_t.shape
    g, hh, lc, p = S5_GROUPS, S5_GROUP, S5_CHUNK, S5_STATE
    nc = t // lc
    grp = lambda n0, n1: pl.BlockSpec((None, n0, n1), lambda i: (i, 0, 0))
    chan = pl.BlockSpec((None, hh, nc, lc), lambda i: (i, 0, 0, 0))
    out = pl.pallas_call(
        _s5_kernel,
        out_shape=jax.ShapeDtypeStruct((g, hh, nc, lc), BF16),
        grid=(g,),
        in_specs=[chan, grp(hh, hh * LANES), grp(hh * lc, 2 * p), grp(2 * p, hh * LANES),
                  grp(2, p), grp(1, hh * LANES)],
        out_specs=chan,
        scratch_shapes=[pltpu.VMEM((hh * lc, hh * LANES), BF16)] + [pltpu.VMEM((nc, p), F32)] * 4,
        compiler_params=_params(("parallel",)),
        name="s5_scan",
    )(u_t.reshape(g, hh, nc, lc), k_rows, w_state, w_out, a_chunk, d_row)
    return out.reshape(width, t)


def _glu_kernel(y_ref, w_ref, o_ref):
    y = y_ref[...]
    a = jnp.dot(w_ref[...], y, preferred_element_type=F32)
    o_ref[...] = (y.astype(F32) * jax.nn.sigmoid(a)).T.astype(o_ref.dtype)


def s5_glu(y_t, w_glu_t):
    n, t = y_t.shape
    tn = min(1024, t)
    return pl.pallas_call(
        _glu_kernel,
        out_shape=jax.ShapeDtypeStruct((t, n), BF16),
        grid=(t // tn,),
        in_specs=[pl.BlockSpec((n, tn), lambda i: (0, i)),
                  pl.BlockSpec((n, n), lambda i: (0, 0))],
        out_specs=pl.BlockSpec((tn, n), lambda i: (i, 0)),
        compiler_params=_params(("parallel",)),
        name="s5_glu",
    )(y_t, w_glu_t)


ML_EXT = ML_HEAD_DIM + LANES


def _mlstm_kernel(q_ref, k_ref, v_ref, og_ref, g_ref, cw_ref, bias_ref, o_ref,
                  qbuf, kbuf, st_scr, m_scr):
    L = ML_CHUNK
    dh = ML_HEAD_DIM
    halo = SUBLANES
    c = pl.program_id(0)

    @pl.when(c == 0)
    def _():
        qbuf[0:halo, :] = jnp.zeros((halo, ML_WIDTH), F32)
        kbuf[0:halo, :] = jnp.zeros((halo, ML_WIDTH), F32)
        st_scr[...] = jnp.zeros(st_scr.shape, F32)
        m_scr[...] = jnp.zeros(m_scr.shape, F32)

    qbuf[halo:halo + L, :] = q_ref[...].astype(F32)
    kbuf[halo:halo + L, :] = k_ref[...].astype(F32)

    gp = g_ref[...] + bias_ref[...]
    lane = lax.broadcasted_iota(jnp.int32, gp.shape, 1)
    gl = jnp.where(lane < ML_HEADS, gp, _log_sigmoid(gp))
    r_i = lax.broadcasted_iota(jnp.int32, (L, L), 0)
    c_i = lax.broadcasted_iota(jnp.int32, (L, L), 1)
    tril = r_i >= c_i
    csum = jnp.dot(jnp.where(tril, 1.0, 0.0), gl, precision=HIGHEST,
                   preferred_element_type=F32)
    gl_t = gl.T
    csum_t = csum.T
    ones_col = (lax.broadcasted_iota(jnp.int32, (L, LANES), 1) == 0).astype(F32)

    for h in range(ML_HEADS):
        cs = slice(h * dh, (h + 1) * dh)

        def conv(buf, part):
            acc = None
            for j in range(ML_CONV):
                off = halo - (ML_CONV - 1) + j
                term = buf[off:off + L, cs] * cw_ref[j:j + 1, part * ML_WIDTH + h * dh:part * ML_WIDTH + (h + 1) * dh]
                acc = term if acc is None else acc + term
            return acc * jax.nn.sigmoid(acc)

        qh = conv(qbuf, 0)
        kh = conv(kbuf, 1) * (dh ** -0.5)
        vh = v_ref[:, cs]
        v_ext = jnp.concatenate([vh.astype(F32), ones_col], axis=1)

        li_col = gl[:, h:h + 1]
        b_col = csum[:, ML_HEADS + h:ML_HEADS + h + 1]
        li_row = gl_t[h:h + 1, :]
        b_row = csum_t[ML_HEADS + h:ML_HEADS + h + 1, :]
        b_tot = b_row[:, L - 1:L]
        m0 = m_scr[h:h + 1, 0:1]

        a_col = b_tot - b_col + li_col
        m_loc = jnp.max(a_col, axis=0, keepdims=True)
        wa = jnp.exp(a_col - m_loc)

        qb = qh.astype(BF16)
        qk = lax.dot_general(qb, kh.astype(BF16), (((1,), (1,)), ((), ())),
                             preferred_element_type=F32)
        log_d = jnp.where(tril, b_col - b_row + li_row, -jnp.inf)
        inter_log = b_col + m0
        m_t = jnp.maximum(inter_log, jnp.max(log_d, axis=1, keepdims=True))
        s_mat = jnp.exp(log_d - m_t) * qk
        w_inter = jnp.exp(inter_log - m_t)
        state = st_scr[h]
        nd = (jnp.dot(s_mat.astype(BF16), v_ext.astype(BF16), preferred_element_type=F32)
              + w_inter * jnp.dot(qb, state.astype(BF16), preferred_element_type=F32))
        num = nd[:, :dh]
        den = nd[:, dh:dh + 1]
        hval = num / jnp.maximum(jnp.abs(den), jnp.exp(-m_t))
        gate = jax.nn.sigmoid(og_ref[:, cs].astype(F32))
        o_ref[:, cs] = (hval * gate).astype(o_ref.dtype)

        m_new = jnp.maximum(b_tot + m0, m_loc)
        s_old = jnp.exp(b_tot + m0 - m_new)
        s_new = jnp.exp(m_loc - m_new)
        upd = jnp.dot(kh.T.astype(BF16), (wa * v_ext).astype(BF16), preferred_element_type=F32)
        st_scr[h] = s_old * state + s_new * upd
        m_scr[h:h + 1, :] = jnp.broadcast_to(m_new, (1, LANES))

    qbuf[0:halo, :] = qbuf[L:L + halo, :]
    kbuf[0:halo, :] = kbuf[L:L + halo, :]


def mlstm(proj, gates, conv_w, i_bias, f_bias):
    t = proj.shape[0]
    L = ML_CHUNK
    wb = ML_WIDTH
    bias = jnp.zeros((1, LANES), F32).at[0, :ML_HEADS].set(i_bias).at[0, ML_HEADS:2 * ML_HEADS].set(f_bias)
    col = lambda off: pl.BlockSpec((L, wb), lambda c, off=off: (c, off // wb))
    return pl.pallas_call(
        _mlstm_kernel,
        out_shape=jax.ShapeDtypeStruct((t, wb), BF16),
        grid=(t // L,),
        in_specs=[col(COL_ML_Q), col(COL_ML_K), col(COL_ML_V), col(COL_ML_O),
                  pl.BlockSpec((L, LANES), lambda c: (c, 0)),
                  pl.BlockSpec((ML_CONV, 2 * wb), lambda c: (0, 0)),
                  pl.BlockSpec((1, LANES), lambda c: (0, 0))],
        out_specs=pl.BlockSpec((L, wb), lambda c: (c, 0)),
        scratch_shapes=[pltpu.VMEM((L + SUBLANES, wb), F32), pltpu.VMEM((L + SUBLANES, wb), F32),
                        pltpu.VMEM((ML_HEADS, ML_HEAD_DIM, ML_EXT), F32),
                        pltpu.VMEM((SUBLANES, LANES), F32)],
        compiler_params=_params(("arbitrary",)),
        name="mlstm",
    )(proj, proj, proj, proj, gates, conv_w, bias)


def _merge_kernel(ysb_ref, ys5_ref, yml_ref, psb_ref, ps5_ref, pml_ref,
                  gsb_ref, gs5_ref, gml_ref, o_ref):
    def branch(y_ref, p_ref, g_ref):
        return (jax.nn.sigmoid(g_ref[...].astype(F32))
                * jnp.dot(y_ref[...], p_ref[...], preferred_element_type=F32))

    o_ref[...] = (branch(ysb_ref, psb_ref, gsb_ref) + branch(ys5_ref, ps5_ref, gs5_ref)
                  + branch(yml_ref, pml_ref, gml_ref)).astype(o_ref.dtype)


def merge(y_sb, y_s5, y_ml, p_sb, p_s5, p_ml, proj):
    t, k = y_sb.shape
    n = p_sb.shape[1]
    tm, tn = min(1024, t), 512
    ysp = pl.BlockSpec((tm, k), lambda i, j: (i, 0))
    psp = pl.BlockSpec((k, tn), lambda i, j: (0, j))
    gsp = lambda off: pl.BlockSpec((tm, tn), lambda i, j, off=off: (i, off // tn + j))
    return pl.pallas_call(
        _merge_kernel,
        out_shape=jax.ShapeDtypeStruct((t, n), BF16),
        grid=(t // tm, n // tn),
        in_specs=[ysp, ysp, ysp, psp, psp, psp, gsp(COL_G_SB), gsp(COL_G_S5), gsp(COL_G_ML)],
        out_specs=pl.BlockSpec((tm, tn), lambda i, j: (i, j)),
        compiler_params=_params(("parallel", "parallel")),
        name="merge",
    )(y_sb, y_s5, y_ml, p_sb, p_s5, p_ml, proj, proj, proj)


def _route(logits):
    lane = lax.broadcasted_iota(jnp.int32, logits.shape, 1)
    neg = -jnp.inf
    big = jnp.int32(LANES)
    is_g = jnp.logical_and(lane >= N_EXPERTS, lane < N_EXPERTS + N_GROUPS)
    gl = jnp.where(is_g, logits, neg)
    gmax = jnp.max(gl, axis=1, keepdims=True)
    g_idx = jnp.min(jnp.where(gl == gmax, lane, big), axis=1, keepdims=True) - N_EXPERTS
    g_w = 1.0 / jnp.sum(jnp.where(is_g, jnp.exp(gl - gmax), 0.0), axis=1, keepdims=True)
    in_grp = jnp.logical_and(lane >= g_idx * EXPERTS_PER_GROUP, lane < (g_idx + 1) * EXPERTS_PER_GROUP)
    e1 = jnp.where(in_grp, logits, neg)
    v1 = jnp.max(e1, axis=1, keepdims=True)
    i1 = jnp.min(jnp.where(e1 == v1, lane, big), axis=1, keepdims=True)
    e2 = jnp.where(lane == i1, neg, e1)
    v2 = jnp.max(e2, axis=1, keepdims=True)
    i2 = jnp.min(jnp.where(e2 == v2, lane, big), axis=1, keepdims=True)
    ex = jnp.exp(v2 - v1)
    return i1, i2, g_w / (1.0 + ex), g_w * ex / (1.0 + ex)


META_E1, META_E2, META_W1, META_W2, META_POS1, META_POS2 = range(6)


def _outproj_kernel(m_ref, w_ref, x_ref, g1_ref, ng_ref, sh_ref, sc_ref, wr_ref, br_ref,
                    xo_ref, h_ref, meta_ref, cnt_ref, carry):
    @pl.when(pl.program_id(0) == 0)
    def _():
        carry[...] = jnp.zeros(carry.shape, F32)

    y = jnp.dot(m_ref[...], w_ref[...], preferred_element_type=F32)
    x = x_ref[...] + g1_ref[...] * y
    xo_ref[...] = x
    h = _norm_mod(x, ng_ref[...], sh_ref[...], sc_ref[...])
    h_ref[...] = h
    logits = jnp.dot(h.astype(BF16), wr_ref[...], preferred_element_type=F32) + br_ref[...]
    i1, i2, w1, w2 = _route(logits)

    tm = logits.shape[0]
    lane = lax.broadcasted_iota(jnp.int32, logits.shape, 1)
    picked = jnp.logical_or(lane == i1, lane == i2)
    r_i = lax.broadcasted_iota(jnp.int32, (tm, tm), 0)
    c_i = lax.broadcasted_iota(jnp.int32, (tm, tm), 1)
    earlier = jnp.where(r_i > c_i, 1.0, 0.0).astype(BF16)
    before = jnp.dot(earlier, jnp.where(picked, 1.0, 0.0).astype(BF16), preferred_element_type=F32)
    before = before + carry[0:1, :]
    pos1 = jnp.sum(jnp.where(lane == i1, before, 0.0), axis=1, keepdims=True)
    pos2 = jnp.sum(jnp.where(lane == i2, before, 0.0), axis=1, keepdims=True)
    total = carry[0:1, :] + jnp.sum(jnp.where(picked, 1.0, 0.0), axis=0, keepdims=True)
    carry[...] = jnp.broadcast_to(total, carry.shape)
    cnt_ref[...] = jnp.broadcast_to(total, cnt_ref.shape)

    meta = jnp.zeros(logits.shape, F32)
    for k, val in ((META_E1, i1.astype(F32)), (META_E2, i2.astype(F32)), (META_W1, w1), (META_W2, w2),
                   (META_POS1, pos1), (META_POS2, pos2)):
        meta = jnp.where(lane == k, val, meta)
    meta_ref[...] = meta


def out_proj(merged, w_out, x, gate1, norm_g, shift, scale, w_route, b_route):
    t, d = x.shape
    tm = min(256, t)
    row = pl.BlockSpec((tm, d), lambda i: (i, 0))
    vec = pl.BlockSpec((1, d), lambda i: (0, 0))
    return pl.pallas_call(
        _outproj_kernel,
        out_shape=(jax.ShapeDtypeStruct((t, d), F32), jax.ShapeDtypeStruct((t, d), F32),
                   jax.ShapeDtypeStruct((t, LANES), F32), jax.ShapeDtypeStruct((SUBLANES, LANES), F32)),
        grid=(t // tm,),
        in_specs=[row, pl.BlockSpec((d, d), lambda i: (0, 0)), row, vec, vec, vec, vec,
                  pl.BlockSpec((d, LANES), lambda i: (0, 0)), pl.BlockSpec((1, LANES), lambda i: (0, 0))],
        out_specs=(row, row, pl.BlockSpec((tm, LANES), lambda i: (i, 0)),
                   pl.BlockSpec((SUBLANES, LANES), lambda i: (0, 0))),
        scratch_shapes=[pltpu.VMEM((SUBLANES, LANES), F32)],
        compiler_params=_params(("arbitrary",)),
        name="out_proj",
    )(merged, w_out, x, gate1, norm_g, shift, scale, w_route, b_route)


def moe_plan(meta, counts, n_tiles):
    tile = MOE_TILE
    as_int = lambda k: meta[:, k].astype(jnp.int32)
    cnt = counts[0, :N_EXPERTS].astype(jnp.int32)
    padded = ((cnt + tile - 1) // tile) * tile
    seg_end = jnp.cumsum(padded)
    seg_off = seg_end - padded
    n_used = (seg_end[-1] // tile).reshape(1)
    starts = jnp.arange(n_tiles, dtype=jnp.int32) * tile
    tile_expert = jnp.sum((starts[:, None] >= seg_end[None, :]).astype(jnp.int32), axis=1)
    tile_expert = jnp.minimum(tile_expert, N_EXPERTS - 1)
    picks = (as_int(META_E1), as_int(META_E2), as_int(META_POS1), as_int(META_POS2), seg_off)
    return picks, tile_expert, n_used


def _row_copy(src, src_row, dst, dst_row, sem):
    return pltpu.make_async_copy(src.at[pl.ds(src_row, 1)], dst.at[pl.ds(dst_row, 1)], sem)


def _experts_kernel(e1_ref, e2_ref, p1_ref, p2_ref, off_ref, te_ref, nu_ref,
                    h_ref, w1_ref, w3_ref, w2_ref, y_ref, token_of, buf, sems, w1b, w3b, w2b):
    j = pl.program_id(0)
    tile = MOE_TILE
    n_used = nu_ref[0]

    @pl.when(j == 0)
    def _():
        def clear(s, carry):
            token_of[s] = 0
            return carry

        def fill(t, carry):
            token_of[off_ref[e1_ref[t]] + p1_ref[t]] = t
            token_of[off_ref[e2_ref[t]] + p2_ref[t]] = t
            return carry

        lax.fori_loop(0, token_of.shape[0], clear, 0, unroll=8)
        lax.fori_loop(0, e1_ref.shape[0], fill, 0, unroll=4)

        def first(r, carry):
            _row_copy(h_ref, token_of[r], buf.at[0], r, sems.at[0]).start()
            return carry

        lax.fori_loop(0, tile, first, 0, unroll=4)

    def tile_done(slot):
        return pltpu.make_async_copy(h_ref.at[pl.ds(0, tile)], buf.at[slot], sems.at[slot])

    def run_tile(slot):
        base = jnp.minimum(j + 1, n_used - 1) * tile
        for r in range(tile):
            _row_copy(h_ref, token_of[base + r], buf.at[1 - slot], r, sems.at[1 - slot]).start()
        tile_done(slot).wait()
        x = buf[slot].astype(BF16)
        a = jnp.dot(x, w1b[...], preferred_element_type=F32)
        b = jnp.dot(x, w3b[...], preferred_element_type=F32)
        hid = (a * jax.nn.sigmoid(a) * b).astype(BF16)
        y_ref[...] = jnp.dot(hid, w2b[...], preferred_element_type=F32)

        @pl.when(j == n_used - 1)
        def _():
            tile_done(1 - slot).wait()

    @pl.when(j < n_used)
    def _():
        @pl.when(jnp.logical_or(j == 0, te_ref[j] != te_ref[jnp.maximum(j - 1, 0)]))
        def _():
            w1b[...] = w1_ref[...].astype(BF16)
            w3b[...] = w3_ref[...].astype(BF16)
            w2b[...] = w2_ref[...].astype(BF16)

        for slot in range(2):
            pl.when(j % 2 == slot)(functools.partial(run_tile, slot))

    @pl.when(j >= n_used)
    def _():
        y_ref[...] = jnp.zeros(y_ref.shape, y_ref.dtype)


def moe_experts(h, picks, tile_expert, n_used, w1, w3, w2, n_slots):
    t, d = h.shape
    ne, _, ff = w1.shape
    tile = MOE_TILE
    n_pref = len(picks) + 2

    def wspec(r, c):
        def index(j, *pref):
            te, nu = pref[-2], pref[-1]
            return (te[jnp.minimum(j, nu[0] - 1)], 0, 0)
        return pl.BlockSpec((None, r, c), index)

    return pl.pallas_call(
        _experts_kernel,
        out_shape=jax.ShapeDtypeStruct((n_slots, d), F32),
        grid_spec=pltpu.PrefetchScalarGridSpec(
            num_scalar_prefetch=n_pref,
            grid=(n_slots // tile,),
            in_specs=[pl.BlockSpec(memory_space=pl.ANY), wspec(d, ff), wspec(d, ff), wspec(ff, d)],
            out_specs=pl.BlockSpec((tile, d), lambda j, *pref: (j, 0)),
            scratch_shapes=[pltpu.SMEM((n_slots,), jnp.int32), pltpu.VMEM((2, tile, d), F32),
                            pltpu.SemaphoreType.DMA((2,)),
                            pltpu.VMEM((d, ff), BF16), pltpu.VMEM((d, ff), BF16), pltpu.VMEM((ff, d), BF16)]),
        compiler_params=_params(("arbitrary",)),
        name="moe_experts",
    )(*picks, tile_expert, n_used, h, w1, w3, w2)


def _combine_kernel(e1_ref, e2_ref, p1_ref, p2_ref, off_ref, y_ref, x_ref, meta_ref, g2_ref, fg_ref,
                    o_ref, buf, sems, *, final_norm):
    i = pl.program_id(0)
    tm = x_ref.shape[0]

    def gather(tile_idx, slot):
        base = tile_idx * tm

        def issue(r, carry):
            t = base + r
            _row_copy(y_ref, off_ref[e1_ref[t]] + p1_ref[t], buf.at[slot, 0], r, sems.at[slot]).start()
            _row_copy(y_ref, off_ref[e2_ref[t]] + p2_ref[t], buf.at[slot, 1], r, sems.at[slot]).start()
            return carry

        lax.fori_loop(0, tm, issue, 0, unroll=4)

    @pl.when(i == 0)
    def _():
        gather(0, 0)

    @pl.when(i + 1 < pl.num_programs(0))
    def _():
        gather(i + 1, (i + 1) % 2)

    slot = i % 2
    for k in range(2):
        pltpu.make_async_copy(y_ref.at[pl.ds(0, tm)], buf.at[slot, k], sems.at[slot]).wait()

    meta = meta_ref[...]
    mix = meta[:, META_W1:META_W1 + 1] * buf[slot, 0] + meta[:, META_W2:META_W2 + 1] * buf[slot, 1]
    x = x_ref[...] + g2_ref[...] * mix
    if final_norm:
        ms = jnp.mean(x * x, axis=-1, keepdims=True)
        x = x * lax.rsqrt(ms + EPS) * fg_ref[...]
    o_ref[...] = x


def moe_combine(ys, picks, x, meta, gate2, final_g, final_norm):
    t, d = x.shape
    tm = min(256, t)
    row = pl.BlockSpec((tm, d), lambda i, *pref: (i, 0))
    vec = pl.BlockSpec((1, d), lambda i, *pref: (0, 0))
    return pl.pallas_call(
        functools.partial(_combine_kernel, final_norm=final_norm),
        out_shape=jax.ShapeDtypeStruct((t, d), F32),
        grid_spec=pltpu.PrefetchScalarGridSpec(
            num_scalar_prefetch=len(picks),
            grid=(t // tm,),
            in_specs=[pl.BlockSpec(memory_space=pl.ANY), row,
                      pl.BlockSpec((tm, LANES), lambda i, *pref: (i, 0)), vec, vec],
            out_specs=row,
            scratch_shapes=[pltpu.VMEM((2, 2, tm, d), F32), pltpu.SemaphoreType.DMA((2,))]),
        compiler_params=_params(("arbitrary",)),
        name="moe_combine",
    )(*picks, ys, x, meta, gate2, final_g)


def kernel(x, c, norm_mix_g, norm_moe_g, final_g, w_ada, b_ada, w_in, ml_conv, ml_i_bias, ml_f_bias,
           s5_lam_re, s5_lam_im, s5_log_dt, s5_b_re, s5_b_im, s5_c_re, s5_c_im, s5_d, s5_w_glu,
           p_sb, p_s5, p_ml, w_out, moe_w_group, moe_b_group, moe_w_router, moe_b_router,
           moe_w1, moe_w3, moe_w2):
    bsz, seq, d = x.shape
    assert bsz == 1 and d == D_MODEL
    assert seq % max(ML_CHUNK, SB_BLOCK, S5_CHUNK * SUBLANES) == 0
    depth = w_in.shape[0]
    xt = x.reshape(seq, d)
    mod = ada_mod(c, w_ada, b_ada)
    for l in range(depth):
        shift1, scale1, gate1, shift2, scale2, gate2 = [mod[l, :, i * d:(i + 1) * d] for i in range(6)]
        w_l = w_in[l]
        w_main = jnp.concatenate([w_l[:, :WIN_S5_U], w_l[:, WIN_ML_Q:WIN_GATES], w_l[:, WIN_G_SB:]],
                                 axis=1).astype(BF16)
        w_gate = jnp.pad(w_l[:, WIN_GATES:WIN_G_SB], ((0, 0), (0, LANES - N_GATE_COLS))).astype(BF16)
        w_s5_t = w_l[:, WIN_S5_U:WIN_ML_Q].T.astype(BF16)
        proj, gates, u_t = in_proj(xt, norm_mix_g[l].reshape(1, d), shift1, scale1, w_main, w_gate, w_s5_t)

        y_sb = sb_attention(proj)

        y_t = s5_scan(u_t, *s5_weights(s5_lam_re[l], s5_lam_im[l], s5_log_dt[l], s5_b_re[l], s5_b_im[l],
                                       s5_c_re[l], s5_c_im[l], s5_d[l]))
        y_s5 = s5_glu(y_t, s5_w_glu[l].T.astype(BF16))

        y_ml = mlstm(proj, gates, ml_conv[l], ml_i_bias[l], ml_f_bias[l])

        merged = merge(y_sb, y_s5, y_ml, p_sb[l].astype(BF16), p_s5[l].astype(BF16), p_ml[l].astype(BF16), proj)

        w_route = jnp.pad(jnp.concatenate([moe_w_router[l], moe_w_group[l]], axis=1),
                          ((0, 0), (0, LANES - N_EXPERTS - N_GROUPS))).astype(BF16)
        b_route = jnp.pad(jnp.concatenate([moe_b_router[l], moe_b_group[l]]),
                          (0, LANES - N_EXPERTS - N_GROUPS)).reshape(1, LANES)
        xt, h2, meta, counts = out_proj(merged, w_out[l].astype(BF16), xt, gate1, norm_moe_g[l].reshape(1, d),
                                        shift2, scale2, w_route, b_route)
        n_slots = 2 * seq + N_EXPERTS * MOE_TILE
        picks, tile_expert, n_used = moe_plan(meta, counts, n_slots // MOE_TILE)
        ys = moe_experts(h2, picks, tile_expert, n_used, moe_w1[l], moe_w3[l], moe_w2[l], n_slots)
        xt = moe_combine(ys, picks, xt, meta, gate2, final_g.reshape(1, d), final_norm=(l == depth - 1))
    return xt.reshape(bsz, seq, d)
```

```python
import functools
import math

import jax
import jax.numpy as jnp
from jax import lax
from jax.experimental import pallas as pl
from jax.experimental.pallas import tpu as pltpu

F32 = jnp.float32
BF16 = jnp.bfloat16
HIGHEST = lax.Precision.HIGHEST

D_MODEL = 2048
DEPTH = 2
SB_HEADS = 8
SB_HEAD_DIM = 128
SB_WIDTH = SB_HEADS * SB_HEAD_DIM
S5_WIDTH = 1024
S5_GROUP = 16
S5_GROUPS = S5_WIDTH // S5_GROUP
S5_STATE = 64
ML_HEADS = 4
ML_HEAD_DIM = 256
ML_WIDTH = ML_HEADS * ML_HEAD_DIM
ML_CONV = 4
N_GROUPS = 4
EXPERTS_PER_GROUP = 8
N_EXPERTS = N_GROUPS * EXPERTS_PER_GROUP
EXPERT_FF = 256
EPS = 1e-6

LANES = 128
SUBLANES = 8
VMEM_LIMIT = 52 * 1024 * 1024

WIN_S5_U = 3072
WIN_ML_Q = 4096
WIN_GATES = 8192
N_GATE_COLS = 2 * ML_HEADS
WIN_G_SB = WIN_GATES + N_GATE_COLS
COL_SB_Q, COL_SB_K, COL_SB_V = 0, 1024, 2048
COL_ML_Q, COL_ML_K, COL_ML_V, COL_ML_O = 3072, 4096, 5120, 6144
COL_G_SB, COL_G_S5, COL_G_ML = 7168, 9216, 11264

S5_CHUNK = 64
ML_CHUNK = 256
SB_BLOCK = 256
SB_HEADS_PER_STEP = 2
MOE_TILE = 256
SB_LOG_CUTOFF = -105.0


def _params(sem, vmem=VMEM_LIMIT):
    return pltpu.CompilerParams(dimension_semantics=sem, vmem_limit_bytes=vmem)


def _log_sigmoid(z):
    return jnp.minimum(z, 0.0) - jnp.log1p(jnp.exp(-jnp.abs(z)))


def _ada_kernel(c_ref, w_ref, b_ref, o_ref):
    c = c_ref[...]
    ca = c * jax.nn.sigmoid(c)
    r = jnp.dot(ca, w_ref[...], precision=HIGHEST, preferred_element_type=F32)
    o_ref[...] = r[0:1, :] + b_ref[...]


def ada_mod(c, w_ada, b_ada):
    depth, d, n = w_ada.shape
    tn = 1024
    c8 = jnp.broadcast_to(c.reshape(1, d), (SUBLANES, d))
    return pl.pallas_call(
        _ada_kernel,
        out_shape=jax.ShapeDtypeStruct((depth, 1, n), F32),
        grid=(depth, n // tn),
        in_specs=[pl.BlockSpec((SUBLANES, d), lambda l, j: (0, 0)),
                  pl.BlockSpec((None, d, tn), lambda l, j: (l, 0, j)),
                  pl.BlockSpec((None, 1, tn), lambda l, j: (l, 0, j))],
        out_specs=pl.BlockSpec((None, 1, tn), lambda l, j: (l, 0, j)),
        compiler_params=_params(("parallel", "parallel")),
        name="ada_mod",
    )(c8, w_ada, b_ada.reshape(depth, 1, n))


def _norm_mod(x, g, shift, scale):
    ms = jnp.mean(x * x, axis=-1, keepdims=True)
    y = x * lax.rsqrt(ms + EPS) * g
    return y * (1.0 + scale) + shift


def _inproj_kernel(x_ref, g_ref, sh_ref, sc_ref, w_ref, wg_ref, wu_ref, o_ref, og_ref, ou_ref, h_scr):
    @pl.when(pl.program_id(1) == 0)
    def _():
        hb = _norm_mod(x_ref[...], g_ref[...], sh_ref[...], sc_ref[...]).astype(BF16)
        h_scr[...] = hb
        og_ref[...] = jnp.dot(hb, wg_ref[...], preferred_element_type=F32)
        ou_ref[...] = lax.dot_general(wu_ref[...], hb, (((1,), (1,)), ((), ())),
                                      preferred_element_type=F32).astype(ou_ref.dtype)

    o_ref[...] = jnp.dot(h_scr[...], w_ref[...], preferred_element_type=F32).astype(o_ref.dtype)


def in_proj(x, g, shift, scale, w_main, w_gate, w_s5_t):
    t, d = x.shape
    n = w_main.shape[1]
    nu = w_s5_t.shape[0]
    tm = min(1024, t)
    tn = 512
    vec = pl.BlockSpec((1, d), lambda i, j: (0, 0))
    return pl.pallas_call(
        _inproj_kernel,
        out_shape=(jax.ShapeDtypeStruct((t, n), BF16), jax.ShapeDtypeStruct((t, LANES), F32),
                   jax.ShapeDtypeStruct((nu, t), BF16)),
        grid=(t // tm, n // tn),
        in_specs=[pl.BlockSpec((tm, d), lambda i, j: (i, 0)), vec, vec, vec,
                  pl.BlockSpec((d, tn), lambda i, j: (0, j)),
                  pl.BlockSpec((d, LANES), lambda i, j: (0, 0)),
                  pl.BlockSpec((nu, d), lambda i, j: (0, 0))],
        out_specs=(pl.BlockSpec((tm, tn), lambda i, j: (i, j)),
                   pl.BlockSpec((tm, LANES), lambda i, j: (i, 0)),
                   pl.BlockSpec((nu, tm), lambda i, j: (0, i))),
        scratch_shapes=[pltpu.VMEM((tm, d), BF16)],
        compiler_params=_params(("parallel", "arbitrary")),
        name="in_proj",
    )(x, g, shift, scale, w_main, w_gate, w_s5_t)


def _sb_kernel(q_ref, k_ref, v_ref, o_ref, acc_ref, run_ref):
    blk = SB_BLOCK
    hd = SB_HEAD_DIM
    i = pl.program_id(1)
    scale = hd ** -0.5
    row = lax.broadcasted_iota(jnp.int32, (blk, blk), 0)
    col = lax.broadcasted_iota(jnp.int32, (blk, blk), 1)
    after = jnp.where(row > col, 1.0, 0.0).astype(BF16)
    causal = col < row
    acc_ref[...] = jnp.zeros(acc_ref.shape, F32)
    run_ref[...] = jnp.zeros(run_ref.shape, F32)

    def step(kb, diagonal):
        start = pl.multiple_of(kb * blk, blk)
        worst = None
        for h in range(SB_HEADS_PER_STEP):
            cs = slice(h * hd, (h + 1) * hd)
            z = lax.dot_general(q_ref[:, cs], k_ref[pl.ds(start, blk), cs], (((1,), (1,)), ((), ())),
                                preferred_element_type=F32) * scale
            softplus = jnp.maximum(z, 0.0) + jnp.log(1.0 + jnp.exp(-jnp.abs(z)))
            log_keep = -softplus
            log_beta = z - softplus
            if diagonal:
                log_keep = jnp.where(causal, log_keep, 0.0)
            hi = log_keep.astype(BF16)
            lo = (log_keep - hi.astype(F32)).astype(BF16)
            later = (jnp.dot(hi, after, preferred_element_type=F32)
                     + jnp.dot(lo, after, preferred_element_type=F32))
            run = run_ref[h]
            w = jnp.exp(log_beta + later + run)
            if diagonal:
                w = jnp.where(causal, w, 0.0)
            acc_ref[h] += jnp.dot(w.astype(BF16), v_ref[pl.ds(start, blk), cs], preferred_element_type=F32)
            run = run + jnp.sum(log_keep, axis=1, keepdims=True)
            run_ref[h] = run
            top = jnp.max(run)
            worst = top if worst is None else jnp.maximum(worst, top)
        return worst

    worst = step(i, True)

    def cond(carry):
        kb, worst = carry
        return jnp.logical_and(kb >= 0, worst > SB_LOG_CUTOFF)

    def body(carry):
        kb, _ = carry
        return kb - 1, step(kb, False)

    lax.while_loop(cond, body, (i - 1, worst))
    for h in range(SB_HEADS_PER_STEP):
        o_ref[:, h * hd:(h + 1) * hd] = acc_ref[h].astype(o_ref.dtype)


def sb_attention(proj):
    t = proj.shape[0]
    blk = SB_BLOCK
    wide = SB_HEADS_PER_STEP * SB_HEAD_DIM
    qb, kb, vb = COL_SB_Q // wide, COL_SB_K // wide, COL_SB_V // wide
    return pl.pallas_call(
        _sb_kernel,
        out_shape=jax.ShapeDtypeStruct((t, SB_WIDTH), BF16),
        grid=(SB_HEADS // SB_HEADS_PER_STEP, t // blk),
        in_specs=[pl.BlockSpec((blk, wide), lambda h, i: (i, qb + h)),
                  pl.BlockSpec((t, wide), lambda h, i: (0, kb + h)),
                  pl.BlockSpec((t, wide), lambda h, i: (0, vb + h))],
        out_specs=pl.BlockSpec((blk, wide), lambda h, i: (i, h)),
        scratch_shapes=[pltpu.VMEM((SB_HEADS_PER_STEP, blk, SB_HEAD_DIM), F32),
                        pltpu.VMEM((SB_HEADS_PER_STEP, blk, 1), F32)],
        compiler_params=_params(("parallel", "arbitrary")),
        name="sb_attention",
    )(proj, proj, proj)


def s5_weights(lam_re, lam_im, log_dt, b_re, b_im, c_re, c_im, d_skip):
    lc, g, p, hh = S5_CHUNK, S5_GROUPS, S5_STATE, S5_GROUP
    dt = jnp.exp(log_dt)[:, None]
    zr, zi = lam_re * dt, lam_im * dt
    e1 = jnp.exp(zr)
    lbr, lbi = e1 * jnp.cos(zi), e1 * jnp.sin(zi)
    den = lam_re * lam_re + lam_im * lam_im
    fr = ((lbr - 1.0) * lam_re + lbi * lam_im) / den
    fi = (lbi * lam_re - (lbr - 1.0) * lam_im) / den
    bbr = fr[:, :, None] * b_re - fi[:, :, None] * b_im
    bbi = fr[:, :, None] * b_im + fi[:, :, None] * b_re
    tau = jnp.arange(LANES, dtype=F32)
    used = tau < lc

    def c_times_power(shift):
        e = (tau + shift) * used
        mag = jnp.exp(zr[:, :, None, None] * e)
        pr = mag * jnp.cos(zi[:, :, None, None] * e)
        pi = mag * jnp.sin(zi[:, :, None, None] * e)
        cr = c_re.transpose(0, 2, 1)[:, :, :, None]
        ci = c_im.transpose(0, 2, 1)[:, :, :, None]
        return jnp.where(used, cr * pr - ci * pi, 0.0), jnp.where(used, cr * pi + ci * pr, 0.0)

    cp0r, cp0i = c_times_power(0.0)
    k_rows = (jnp.einsum('gpk,gpn->gkn', bbr, cp0r.reshape(g, p, hh * LANES), precision=HIGHEST)
              - jnp.einsum('gpk,gpn->gkn', bbi, cp0i.reshape(g, p, hh * LANES), precision=HIGHEST))
    cp1r, cp1i = c_times_power(1.0)
    w_out = jnp.concatenate([cp1r, -cp1i], axis=1).reshape(g, 2 * p, hh * LANES).astype(BF16)
    rev = (lc - 1.0) - jnp.arange(lc, dtype=F32)[None, :, None]
    rmag = jnp.exp(zr[:, None, :] * rev)
    rev_r, rev_i = rmag * jnp.cos(zi[:, None, :] * rev), rmag * jnp.sin(zi[:, None, :] * rev)
    bt_r, bt_i = bbr.transpose(0, 2, 1), bbi.transpose(0, 2, 1)
    w_state = (jnp.concatenate([rev_r, rev_r], -1)[:, None] * jnp.concatenate([bt_r, bt_i], -1)[:, :, None]
               + jnp.concatenate([-rev_i, rev_i], -1)[:, None] * jnp.concatenate([bt_i, bt_r], -1)[:, :, None]
               ).reshape(g, hh * lc, 2 * p).astype(BF16)
    full = jnp.exp(zr * lc)
    a_chunk = jnp.stack([full * jnp.cos(zi * lc), full * jnp.sin(zi * lc)], axis=1)
    d_row = jnp.broadcast_to(d_skip.reshape(g, 1, hh, 1), (g, 1, hh, LANES)).reshape(g, 1, hh * LANES)
    return k_rows, w_state, w_out, a_chunk, d_row


def _s5_kernel(u_ref, k_ref, ws_ref, wo_ref, a_ref, d_ref, o_ref, w_scr, sr_scr, si_scr, xr_scr, xi_scr):
    p = S5_STATE
    lc = S5_CHUNK
    hh = S5_GROUP
    nc = u_ref.shape[1]
    for hp in range(hh):
        rows = jnp.broadcast_to(k_ref[hp:hp + 1, :], (lc, hh * LANES))
        for h in range(hh):
            cs = slice(h * LANES, (h + 1) * LANES)
            w_scr[hp * lc:(hp + 1) * lc, cs] = pltpu.roll(rows[:, cs], 0, 1, stride=1, stride_axis=0).astype(BF16)

    uf = jnp.concatenate([u_ref[h].astype(F32) for h in range(hh)], axis=1)
    u = uf.astype(BF16)
    s = jnp.dot(u, ws_ref[...], preferred_element_type=F32)
    sr_scr[...] = s[:, :p]
    si_scr[...] = s[:, p:]
    ar = a_ref[0:1, :]
    ai = a_ref[1:2, :]

    def body(c, carry):
        xr, xi = carry
        xr_scr[pl.ds(c, 1), :] = xr
        xi_scr[pl.ds(c, 1), :] = xi
        sr = sr_scr[pl.ds(c, 1), :]
        si = si_scr[pl.ds(c, 1), :]
        return ar * xr - ai * xi + sr, ar * xi + ai * xr + si

    zero = jnp.zeros((1, p), F32)
    lax.fori_loop(0, nc, body, (zero, zero))
    y = jnp.dot(u, w_scr[...], preferred_element_type=F32)
    y = y + jnp.dot(xr_scr[...].astype(BF16), wo_ref[0:p, :], preferred_element_type=F32)
    y = y + jnp.dot(xi_scr[...].astype(BF16), wo_ref[p:2 * p, :], preferred_element_type=F32)
    for h in range(hh):
        cs = slice(h * LANES, h * LANES + lc)
        o_ref[h] = jax.nn.gelu(y[:, cs] + d_ref[:, cs] * u_ref[h].astype(F32)).astype(o_ref.dtype)


def s5_scan(u_t, k_rows, w_state, w_out, a_chunk, d_row):
    width, t = u_t.shape
    g, hh, lc, p = S5_GROUPS, S5_GROUP, S5_CHUNK, S5_STATE
    nc = t // lc
    grp = lambda n0, n1: pl.BlockSpec((None, n0, n1), lambda i: (i, 0, 0))
    chan = pl.BlockSpec((None, hh, nc, lc), lambda i: (i, 0, 0, 0))
    out = pl.pallas_call(
        _s5_kernel,
        out_shape=jax.ShapeDtypeStruct((g, hh, nc, lc), BF16),
        grid=(g,),
        in_specs=[chan, grp(hh, hh * LANES), grp(hh * lc, 2 * p), grp(2 * p, hh * LANES),
                  grp(2, p), grp(1, hh * LANES)],
        out_specs=chan,
        scratch_shapes=[pltpu.VMEM((hh * lc, hh * LANES), BF16)] + [pltpu.VMEM((nc, p), F32)] * 4,
        compiler_params=_params(("parallel",)),
        name="s5_scan",
    )(u_t.reshape(g, hh, nc, lc), k_rows, w_state, w_out, a_chunk, d_row)
    return out.reshape(width, t)


def _glu_kernel(y_ref, w_ref, o_ref):
    y = y_ref[...]
    a = jnp.dot(w_ref[...], y, preferred_element_type=F32)
    o_ref[...] = (y.astype(F32) * jax.nn.sigmoid(a)).T.astype(o_ref.dtype)


def s5_glu(y_t, w_glu_t):
    n, t = y_t.shape
    tn = min(1024, t)
    return pl.pallas_call(
        _glu_kernel,
        out_shape=jax.ShapeDtypeStruct((t, n), BF16),
        grid=(t // tn,),
        in_specs=[pl.BlockSpec((n, tn), lambda i: (0, i)),
                  pl.BlockSpec((n, n), lambda i: (0, 0))],
        out_specs=pl.BlockSpec((tn, n), lambda i: (i, 0)),
        compiler_params=_params(("parallel",)),
        name="s5_glu",
    )(y_t, w_glu_t)


ML_EXT = ML_HEAD_DIM + LANES


def _mlstm_kernel(q_ref, k_ref, v_ref, og_ref, g_ref, cw_ref, bias_ref, o_ref,
                  qbuf, kbuf, st_scr, m_scr):
    L = ML_CHUNK
    dh = ML_HEAD_DIM
    halo = SUBLANES
    c = pl.program_id(0)

    @pl.when(c == 0)
    def _():
        qbuf[0:halo, :] = jnp.zeros((halo, ML_WIDTH), F32)
        kbuf[0:halo, :] = jnp.zeros((halo, ML_WIDTH), F32)
        st_scr[...] = jnp.zeros(st_scr.shape, F32)
        m_scr[...] = jnp.zeros(m_scr.shape, F32)

    qbuf[halo:halo + L, :] = q_ref[...].astype(F32)
    kbuf[halo:halo + L, :] = k_ref[...].astype(F32)

    gp = g_ref[...] + bias_ref[...]
    lane = lax.broadcasted_iota(jnp.int32, gp.shape, 1)
    gl = jnp.where(lane < ML_HEADS, gp, _log_sigmoid(gp))
    r_i = lax.broadcasted_iota(jnp.int32, (L, L), 0)
    c_i = lax.broadcasted_iota(jnp.int32, (L, L), 1)
    tril = r_i >= c_i
    csum = jnp.dot(jnp.where(tril, 1.0, 0.0), gl, precision=HIGHEST,
                   preferred_element_type=F32)
    gl_t = gl.T
    csum_t = csum.T
    ones_col = (lax.broadcasted_iota(jnp.int32, (L, LANES), 1) == 0).astype(F32)

    for h in range(ML_HEADS):
        cs = slice(h * dh, (h + 1) * dh)

        def conv(buf, part):
            acc = None
            for j in range(ML_CONV):
                off = halo - (ML_CONV - 1) + j
                term = buf[off:off + L, cs] * cw_ref[j:j + 1, part * ML_WIDTH + h * dh:part * ML_WIDTH + (h + 1) * dh]
                acc = term if acc is None else acc + term
            return acc * jax.nn.sigmoid(acc)

        qh = conv(qbuf, 0)
        kh = conv(kbuf, 1) * (dh ** -0.5)
        vh = v_ref[:, cs]
        v_ext = jnp.concatenate([vh.astype(F32), ones_col], axis=1)

        li_col = gl[:, h:h + 1]
        b_col = csum[:, ML_HEADS + h:ML_HEADS + h + 1]
        li_row = gl_t[h:h + 1, :]
        b_row = csum_t[ML_HEADS + h:ML_HEADS + h + 1, :]
        b_tot = b_row[:, L - 1:L]
        m0 = m_scr[h:h + 1, 0:1]

        a_col = b_tot - b_col + li_col
        m_loc = jnp.max(a_col, axis=0, keepdims=True)
        wa = jnp.exp(a_col - m_loc)

        qb = qh.astype(BF16)
        qk = lax.dot_general(qb, kh.astype(BF16), (((1,), (1,)), ((), ())),
                             preferred_element_type=F32)
        log_d = jnp.where(tril, b_col - b_row + li_row, -jnp.inf)
        inter_log = b_col + m0
        m_t = jnp.maximum(inter_log, jnp.max(log_d, axis=1, keepdims=True))
        s_mat = jnp.exp(log_d - m_t) * qk
        w_inter = jnp.exp(inter_log - m_t)
        state = st_scr[h]
        nd = (jnp.dot(s_mat.astype(BF16), v_ext.astype(BF16), preferred_element_type=F32)
              + w_inter * jnp.dot(qb, state.astype(BF16), preferred_element_type=F32))
        num = nd[:, :dh]
        den = nd[:, dh:dh + 1]
        hval = num / jnp.maximum(jnp.abs(den), jnp.exp(-m_t))
        gate = jax.nn.sigmoid(og_ref[:, cs].astype(F32))
        o_ref[:, cs] = (hval * gate).astype(o_ref.dtype)

        m_new = jnp.maximum(b_tot + m0, m_loc)
        s_old = jnp.exp(b_tot + m0 - m_new)
        s_new = jnp.exp(m_loc - m_new)
        upd = jnp.dot(kh.T.astype(BF16), (wa * v_ext).astype(BF16), preferred_element_type=F32)
        st_scr[h] = s_old * state + s_new * upd
        m_scr[h:h + 1, :] = jnp.broadcast_to(m_new, (1, LANES))

    qbuf[0:halo, :] = qbuf[L:L + halo, :]
    kbuf[0:halo, :] = kbuf[L:L + halo, :]


def mlstm(proj, gates, conv_w, bias):
    t = proj.shape[0]
    L = ML_CHUNK
    wb = ML_WIDTH
    col = lambda off: pl.BlockSpec((L, wb), lambda c, off=off: (c, off // wb))
    return pl.pallas_call(
        _mlstm_kernel,
        out_shape=jax.ShapeDtypeStruct((t, wb), BF16),
        grid=(t // L,),
        in_specs=[col(COL_ML_Q), col(COL_ML_K), col(COL_ML_V), col(COL_ML_O),
                  pl.BlockSpec((L, LANES), lambda c: (c, 0)),
                  pl.BlockSpec((ML_CONV, 2 * wb), lambda c: (0, 0)),
                  pl.BlockSpec((1, LANES), lambda c: (0, 0))],
        out_specs=pl.BlockSpec((L, wb), lambda c: (c, 0)),
        scratch_shapes=[pltpu.VMEM((L + SUBLANES, wb), F32), pltpu.VMEM((L + SUBLANES, wb), F32),
                        pltpu.VMEM((ML_HEADS, ML_HEAD_DIM, ML_EXT), F32),
                        pltpu.VMEM((SUBLANES, LANES), F32)],
        compiler_params=_params(("arbitrary",)),
        name="mlstm",
    )(proj, proj, proj, proj, gates, conv_w, bias)


def _merge_kernel(ysb_ref, ys5_ref, yml_ref, psb_ref, ps5_ref, pml_ref,
                  gsb_ref, gs5_ref, gml_ref, o_ref):
    def branch(y_ref, p_ref, g_ref):
        return (jax.nn.sigmoid(g_ref[...].astype(F32))
                * jnp.dot(y_ref[...], p_ref[...], preferred_element_type=F32))

    o_ref[...] = (branch(ysb_ref, psb_ref, gsb_ref) + branch(ys5_ref, ps5_ref, gs5_ref)
                  + branch(yml_ref, pml_ref, gml_ref)).astype(o_ref.dtype)


def merge(y_sb, y_s5, y_ml, p_sb, p_s5, p_ml, proj):
    t, k = y_sb.shape
    n = p_sb.shape[1]
    tm, tn = min(1024, t), 512
    ysp = pl.BlockSpec((tm, k), lambda i, j: (i, 0))
    psp = pl.BlockSpec((k, tn), lambda i, j: (0, j))
    gsp = lambda off: pl.BlockSpec((tm, tn), lambda i, j, off=off: (i, off // tn + j))
    return pl.pallas_call(
        _merge_kernel,
        out_shape=jax.ShapeDtypeStruct((t, n), BF16),
        grid=(t // tm, n // tn),
        in_specs=[ysp, ysp, ysp, psp, psp, psp, gsp(COL_G_SB), gsp(COL_G_S5), gsp(COL_G_ML)],
        out_specs=pl.BlockSpec((tm, tn), lambda i, j: (i, j)),
        compiler_params=_params(("parallel", "parallel")),
        name="merge",
    )(y_sb, y_s5, y_ml, p_sb, p_s5, p_ml, proj, proj, proj)


def _route(logits):
    lane = lax.broadcasted_iota(jnp.int32, logits.shape, 1)
    neg = -jnp.inf
    big = jnp.int32(LANES)
    is_g = jnp.logical_and(lane >= N_EXPERTS, lane < N_EXPERTS + N_GROUPS)
    gl = jnp.where(is_g, logits, neg)
    gmax = jnp.max(gl, axis=1, keepdims=True)
    g_idx = jnp.min(jnp.where(gl == gmax, lane, big), axis=1, keepdims=True) - N_EXPERTS
    g_w = 1.0 / jnp.sum(jnp.where(is_g, jnp.exp(gl - gmax), 0.0), axis=1, keepdims=True)
    in_grp = jnp.logical_and(lane >= g_idx * EXPERTS_PER_GROUP, lane < (g_idx + 1) * EXPERTS_PER_GROUP)
    e1 = jnp.where(in_grp, logits, neg)
    v1 = jnp.max(e1, axis=1, keepdims=True)
    i1 = jnp.min(jnp.where(e1 == v1, lane, big), axis=1, keepdims=True)
    e2 = jnp.where(lane == i1, neg, e1)
    v2 = jnp.max(e2, axis=1, keepdims=True)
    i2 = jnp.min(jnp.where(e2 == v2, lane, big), axis=1, keepdims=True)
    ex = jnp.exp(v2 - v1)
    return i1, i2, g_w / (1.0 + ex), g_w * ex / (1.0 + ex)


META_E1, META_E2, META_W1, META_W2, META_POS1, META_POS2 = range(6)


def _outproj_kernel(m_ref, w_ref, x_ref, g1_ref, ng_ref, sh_ref, sc_ref, wr_ref, br_ref,
                    xo_ref, h_ref, meta_ref, cnt_ref, carry):
    @pl.when(pl.program_id(0) == 0)
    def _():
        carry[...] = jnp.zeros(carry.shape, F32)

    y = jnp.dot(m_ref[...], w_ref[...], preferred_element_type=F32)
    x = x_ref[...] + g1_ref[...] * y
    xo_ref[...] = x
    h = _norm_mod(x, ng_ref[...], sh_ref[...], sc_ref[...])
    h_ref[...] = h
    logits = jnp.dot(h.astype(BF16), wr_ref[...], preferred_element_type=F32) + br_ref[...]
    i1, i2, w1, w2 = _route(logits)

    tm = logits.shape[0]
    lane = lax.broadcasted_iota(jnp.int32, logits.shape, 1)
    picked = jnp.logical_or(lane == i1, lane == i2)
    r_i = lax.broadcasted_iota(jnp.int32, (tm, tm), 0)
    c_i = lax.broadcasted_iota(jnp.int32, (tm, tm), 1)
    earlier = jnp.where(r_i > c_i, 1.0, 0.0).astype(BF16)
    before = jnp.dot(earlier, jnp.where(picked, 1.0, 0.0).astype(BF16), preferred_element_type=F32)
    before = before + carry[0:1, :]
    pos1 = jnp.sum(jnp.where(lane == i1, before, 0.0), axis=1, keepdims=True)
    pos2 = jnp.sum(jnp.where(lane == i2, before, 0.0), axis=1, keepdims=True)
    total = carry[0:1, :] + jnp.sum(jnp.where(picked, 1.0, 0.0), axis=0, keepdims=True)
    carry[...] = jnp.broadcast_to(total, carry.shape)
    cnt_ref[...] = jnp.broadcast_to(total, cnt_ref.shape)

    meta = jnp.zeros(logits.shape, F32)
    for k, val in ((META_E1, i1.astype(F32)), (META_E2, i2.astype(F32)), (META_W1, w1), (META_W2, w2),
                   (META_POS1, pos1), (META_POS2, pos2)):
        meta = jnp.where(lane == k, val, meta)
    meta_ref[...] = meta


def out_proj(merged, w_out, x, gate1, norm_g, shift, scale, w_route, b_route):
    t, d = x.shape
    tm = min(256, t)
    row = pl.BlockSpec((tm, d), lambda i: (i, 0))
    vec = pl.BlockSpec((1, d), lambda i: (0, 0))
    return pl.pallas_call(
        _outproj_kernel,
        out_shape=(jax.ShapeDtypeStruct((t, d), F32), jax.ShapeDtypeStruct((t, d), F32),
                   jax.ShapeDtypeStruct((t, LANES), F32), jax.ShapeDtypeStruct((SUBLANES, LANES), F32)),
        grid=(t // tm,),
        in_specs=[row, pl.BlockSpec((d, d), lambda i: (0, 0)), row, vec, vec, vec, vec,
                  pl.BlockSpec((d, LANES), lambda i: (0, 0)), pl.BlockSpec((1, LANES), lambda i: (0, 0))],
        out_specs=(row, row, pl.BlockSpec((tm, LANES), lambda i: (i, 0)),
                   pl.BlockSpec((SUBLANES, LANES), lambda i: (0, 0))),
        scratch_shapes=[pltpu.VMEM((SUBLANES, LANES), F32)],
        compiler_params=_params(("arbitrary",)),
        name="out_proj",
    )(merged, w_out, x, gate1, norm_g, shift, scale, w_route, b_route)


def moe_plan(meta, counts, n_tiles):
    tile = MOE_TILE
    as_int = lambda k: meta[:, k].astype(jnp.int32)
    cnt = counts[0, :N_EXPERTS].astype(jnp.int32)
    padded = ((cnt + tile - 1) // tile) * tile
    seg_end = jnp.cumsum(padded)
    seg_off = seg_end - padded
    n_used = (seg_end[-1] // tile).reshape(1)
    starts = jnp.arange(n_tiles, dtype=jnp.int32) * tile
    tile_expert = jnp.sum((starts[:, None] >= seg_end[None, :]).astype(jnp.int32), axis=1)
    tile_expert = jnp.minimum(tile_expert, N_EXPERTS - 1)
    picks = (as_int(META_E1), as_int(META_E2), as_int(META_POS1), as_int(META_POS2), seg_off)
    return picks, tile_expert, n_used


def _row_copy(src, src_row, dst, dst_row, sem):
    return pltpu.make_async_copy(src.at[pl.ds(src_row, 1)], dst.at[pl.ds(dst_row, 1)], sem)


def _experts_kernel(e1_ref, e2_ref, p1_ref, p2_ref, off_ref, te_ref, nu_ref,
                    h_ref, w1_ref, w3_ref, w2_ref, y_ref, token_of, buf, sems, w1b, w3b, w2b):
    j = pl.program_id(0)
    tile = MOE_TILE
    n_used = nu_ref[0]

    @pl.when(j == 0)
    def _():
        def clear(s, carry):
            token_of[s] = 0
            return carry

        def fill(t, carry):
            token_of[off_ref[e1_ref[t]] + p1_ref[t]] = t
            token_of[off_ref[e2_ref[t]] + p2_ref[t]] = t
            return carry

        lax.fori_loop(0, token_of.shape[0], clear, 0, unroll=8)
        lax.fori_loop(0, e1_ref.shape[0], fill, 0, unroll=4)

        def first(r, carry):
            _row_copy(h_ref, token_of[r], buf.at[0], r, sems.at[0]).start()
            return carry

        lax.fori_loop(0, tile, first, 0, unroll=4)

    def tile_done(slot):
        return pltpu.make_async_copy(h_ref.at[pl.ds(0, tile)], buf.at[slot], sems.at[slot])

    def run_tile(slot):
        base = jnp.minimum(j + 1, n_used - 1) * tile
        for r in range(tile):
            _row_copy(h_ref, token_of[base + r], buf.at[1 - slot], r, sems.at[1 - slot]).start()
        tile_done(slot).wait()
        x = buf[slot].astype(BF16)
        a = jnp.dot(x, w1b[...], preferred_element_type=F32)
        b = jnp.dot(x, w3b[...], preferred_element_type=F32)
        hid = (a * jax.nn.sigmoid(a) * b).astype(BF16)
        y_ref[...] = jnp.dot(hid, w2b[...], preferred_element_type=F32)

        @pl.when(j == n_used - 1)
        def _():
            tile_done(1 - slot).wait()

    @pl.when(j < n_used)
    def _():
        @pl.when(jnp.logical_or(j == 0, te_ref[j] != te_ref[jnp.maximum(j - 1, 0)]))
        def _():
            w1b[...] = w1_ref[...].astype(BF16)
            w3b[...] = w3_ref[...].astype(BF16)
            w2b[...] = w2_ref[...].astype(BF16)

        for slot in range(2):
            pl.when(j % 2 == slot)(functools.partial(run_tile, slot))

    @pl.when(j >= n_used)
    def _():
        y_ref[...] = jnp.zeros(y_ref.shape, y_ref.dtype)


def moe_experts(h, picks, tile_expert, n_used, w1, w3, w2, n_slots):
    t, d = h.shape
    ne, _, ff = w1.shape
    tile = MOE_TILE
    n_pref = len(picks) + 2

    def wspec(r, c):
        def index(j, *pref):
            te, nu = pref[-2], pref[-1]
            return (te[jnp.minimum(j, nu[0] - 1)], 0, 0)
        return pl.BlockSpec((None, r, c), index)

    return pl.pallas_call(
        _experts_kernel,
        out_shape=jax.ShapeDtypeStruct((n_slots, d), F32),
        grid_spec=pltpu.PrefetchScalarGridSpec(
            num_scalar_prefetch=n_pref,
            grid=(n_slots // tile,),
            in_specs=[pl.BlockSpec(memory_space=pl.ANY), wspec(d, ff), wspec(d, ff), wspec(ff, d)],
            out_specs=pl.BlockSpec((tile, d), lambda j, *pref: (j, 0)),
            scratch_shapes=[pltpu.SMEM((n_slots,), jnp.int32), pltpu.VMEM((2, tile, d), F32),
                            pltpu.SemaphoreType.DMA((2,)),
                            pltpu.VMEM((d, ff), BF16), pltpu.VMEM((d, ff), BF16), pltpu.VMEM((ff, d), BF16)]),
        compiler_params=_params(("arbitrary",)),
        name="moe_experts",
    )(*picks, tile_expert, n_used, h, w1, w3, w2)


def _combine_kernel(e1_ref, e2_ref, p1_ref, p2_ref, off_ref, y_ref, x_ref, meta_ref, g2_ref, fg_ref,
                    o_ref, buf, sems, *, final_norm):
    i = pl.program_id(0)
    tm = x_ref.shape[0]

    def gather(tile_idx, slot):
        base = tile_idx * tm

        def issue(r, carry):
            t = base + r
            _row_copy(y_ref, off_ref[e1_ref[t]] + p1_ref[t], buf.at[slot, 0], r, sems.at[slot]).start()
            _row_copy(y_ref, off_ref[e2_ref[t]] + p2_ref[t], buf.at[slot, 1], r, sems.at[slot]).start()
            return carry

        lax.fori_loop(0, tm, issue, 0, unroll=4)

    @pl.when(i == 0)
    def _():
        gather(0, 0)

    @pl.when(i + 1 < pl.num_programs(0))
    def _():
        gather(i + 1, (i + 1) % 2)

    slot = i % 2
    for k in range(2):
        pltpu.make_async_copy(y_ref.at[pl.ds(0, tm)], buf.at[slot, k], sems.at[slot]).wait()

    meta = meta_ref[...]
    mix = meta[:, META_W1:META_W1 + 1] * buf[slot, 0] + meta[:, META_W2:META_W2 + 1] * buf[slot, 1]
    x = x_ref[...] + g2_ref[...] * mix
    if final_norm:
        ms = jnp.mean(x * x, axis=-1, keepdims=True)
        x = x * lax.rsqrt(ms + EPS) * fg_ref[...]
    o_ref[...] = x


def moe_combine(ys, picks, x, meta, gate2, final_g, final_norm):
    t, d = x.shape
    tm = min(256, t)
    row = pl.BlockSpec((tm, d), lambda i, *pref: (i, 0))
    vec = pl.BlockSpec((1, d), lambda i, *pref: (0, 0))
    return pl.pallas_call(
        functools.partial(_combine_kernel, final_norm=final_norm),
        out_shape=jax.ShapeDtypeStruct((t, d), F32),
        grid_spec=pltpu.PrefetchScalarGridSpec(
            num_scalar_prefetch=len(picks),
            grid=(t // tm,),
            in_specs=[pl.BlockSpec(memory_space=pl.ANY), row,
                      pl.BlockSpec((tm, LANES), lambda i, *pref: (i, 0)), vec, vec],
            out_specs=row,
            scratch_shapes=[pltpu.VMEM((2, 2, tm, d), F32), pltpu.SemaphoreType.DMA((2,))]),
        compiler_params=_params(("arbitrary",)),
        name="moe_combine",
    )(*picks, ys, x, meta, gate2, final_g)


def kernel(x, c, norm_mix_g, norm_moe_g, final_g, w_ada, b_ada, w_in, ml_conv, ml_i_bias, ml_f_bias,
           s5_lam_re, s5_lam_im, s5_log_dt, s5_b_re, s5_b_im, s5_c_re, s5_c_im, s5_d, s5_w_glu,
           p_sb, p_s5, p_ml, w_out, moe_w_group, moe_b_group, moe_w_router, moe_b_router,
           moe_w1, moe_w3, moe_w2):
    bsz, seq, d = x.shape
    assert bsz == 1 and d == D_MODEL
    assert seq % max(ML_CHUNK, SB_BLOCK, S5_CHUNK * SUBLANES) == 0
    depth = w_in.shape[0]
    xt = x.reshape(seq, d)
    mod = ada_mod(c, w_ada, b_ada)

    w_main = jnp.concatenate([w_in[:, :, :WIN_S5_U], w_in[:, :, WIN_ML_Q:WIN_GATES], w_in[:, :, WIN_G_SB:]],
                             axis=2).astype(BF16)
    w_gate = jnp.pad(w_in[:, :, WIN_GATES:WIN_G_SB], ((0, 0), (0, 0), (0, LANES - N_GATE_COLS))).astype(BF16)
    w_s5_t = w_in[:, :, WIN_S5_U:WIN_ML_Q].transpose(0, 2, 1).astype(BF16)
    s5_mats = jax.vmap(s5_weights)(s5_lam_re, s5_lam_im, s5_log_dt, s5_b_re, s5_b_im, s5_c_re, s5_c_im, s5_d)
    w_glu_t = s5_w_glu.transpose(0, 2, 1).astype(BF16)
    p_sb_b, p_s5_b, p_ml_b, w_out_b = (a.astype(BF16) for a in (p_sb, p_s5, p_ml, w_out))
    route_pad = LANES - N_EXPERTS - N_GROUPS
    w_route = jnp.pad(jnp.concatenate([moe_w_router, moe_w_group], axis=2),
                      ((0, 0), (0, 0), (0, route_pad))).astype(BF16)
    b_route = jnp.pad(jnp.concatenate([moe_b_router, moe_b_group], axis=1), ((0, 0), (0, route_pad)))[:, None, :]
    ml_bias = jnp.pad(jnp.concatenate([ml_i_bias, ml_f_bias], axis=1), ((0, 0), (0, LANES - N_GATE_COLS)))[:, None, :]

    for l in range(depth):
        shift1, scale1, gate1, shift2, scale2, gate2 = [mod[l, :, i * d:(i + 1) * d] for i in range(6)]
        proj, gates, u_t = in_proj(xt, norm_mix_g[l].reshape(1, d), shift1, scale1, w_main[l], w_gate[l], w_s5_t[l])

        y_sb = sb_attention(proj)

        y_t = s5_scan(u_t, *(m[l] for m in s5_mats))
        y_s5 = s5_glu(y_t, w_glu_t[l])

        y_ml = mlstm(proj, gates, ml_conv[l], ml_bias[l])

        merged = merge(y_sb, y_s5, y_ml, p_sb_b[l], p_s5_b[l], p_ml_b[l], proj)

        xt, h2, meta, counts = out_proj(merged, w_out_b[l], xt, gate1, norm_moe_g[l].reshape(1, d),
                                        shift2, scale2, w_route[l], b_route[l])
        n_slots = 2 * seq + N_EXPERTS * MOE_TILE
        picks, tile_expert, n_used = moe_plan(meta, counts, n_slots // MOE_TILE)
        ys = moe_experts(h2, picks, tile_expert, n_used, moe_w1[l], moe_w3[l], moe_w2[l], n_slots)
        xt = moe_combine(ys, picks, xt, meta, gate2, final_g.reshape(1, d), final_norm=(l == depth - 1))
    return xt.reshape(bsz, seq, d)
```

```python
import functools
import math

import jax
import jax.numpy as jnp
from jax import lax
from jax.experimental import pallas as pl
from jax.experimental.pallas import tpu as pltpu

F32 = jnp.float32
BF16 = jnp.bfloat16
HIGHEST = lax.Precision.HIGHEST

D_MODEL = 2048
DEPTH = 2
SB_HEADS = 8
SB_HEAD_DIM = 128
SB_WIDTH = SB_HEADS * SB_HEAD_DIM
S5_WIDTH = 1024
S5_GROUP = 16
S5_GROUPS = S5_WIDTH // S5_GROUP
S5_STATE = 64
ML_HEADS = 4
ML_HEAD_DIM = 256
ML_WIDTH = ML_HEADS * ML_HEAD_DIM
ML_CONV = 4
N_GROUPS = 4
EXPERTS_PER_GROUP = 8
N_EXPERTS = N_GROUPS * EXPERTS_PER_GROUP
EXPERT_FF = 256
EPS = 1e-6

LANES = 128
SUBLANES = 8
VMEM_LIMIT = 52 * 1024 * 1024

WIN_S5_U = 3072
WIN_ML_Q = 4096
WIN_GATES = 8192
N_GATE_COLS = 2 * ML_HEADS
WIN_G_SB = WIN_GATES + N_GATE_COLS
COL_SB_Q, COL_SB_K, COL_SB_V = 0, 1024, 2048
COL_ML_Q, COL_ML_K, COL_ML_V, COL_ML_O = 3072, 4096, 5120, 6144
COL_G_SB, COL_G_S5, COL_G_ML = 7168, 9216, 11264

S5_CHUNK = 64
ML_CHUNK = 256
SB_BLOCK = 256
SB_HEADS_PER_STEP = 2
MOE_TILE = 256
SB_LOG_CUTOFF = -105.0


def _params(sem, vmem=VMEM_LIMIT):
    return pltpu.CompilerParams(dimension_semantics=sem, vmem_limit_bytes=vmem)


def _log_sigmoid(z):
    return jnp.minimum(z, 0.0) - jnp.log1p(jnp.exp(-jnp.abs(z)))


def _ada_kernel(c_ref, w_ref, b_ref, o_ref):
    c = c_ref[...]
    ca = c * jax.nn.sigmoid(c)
    r = jnp.dot(ca, w_ref[...], precision=HIGHEST, preferred_element_type=F32)
    o_ref[...] = r[0:1, :] + b_ref[...]


def ada_mod(c, w_ada, b_ada):
    depth, d, n = w_ada.shape
    tn = 1024
    c8 = jnp.broadcast_to(c.reshape(1, d), (SUBLANES, d))
    return pl.pallas_call(
        _ada_kernel,
        out_shape=jax.ShapeDtypeStruct((depth, 1, n), F32),
        grid=(depth, n // tn),
        in_specs=[pl.BlockSpec((SUBLANES, d), lambda l, j: (0, 0)),
                  pl.BlockSpec((None, d, tn), lambda l, j: (l, 0, j)),
                  pl.BlockSpec((None, 1, tn), lambda l, j: (l, 0, j))],
        out_specs=pl.BlockSpec((None, 1, tn), lambda l, j: (l, 0, j)),
        compiler_params=_params(("parallel", "parallel")),
        name="ada_mod",
    )(c8, w_ada, b_ada.reshape(depth, 1, n))


def _norm_mod(x, g, shift, scale):
    ms = jnp.mean(x * x, axis=-1, keepdims=True)
    y = x * lax.rsqrt(ms + EPS) * g
    return y * (1.0 + scale) + shift


def _inproj_kernel(x_ref, g_ref, sh_ref, sc_ref, w_ref, wg_ref, wu_ref, o_ref, og_ref, ou_ref, h_scr):
    @pl.when(pl.program_id(1) == 0)
    def _():
        hb = _norm_mod(x_ref[...], g_ref[...], sh_ref[...], sc_ref[...]).astype(BF16)
        h_scr[...] = hb
        og_ref[...] = jnp.dot(hb, wg_ref[...], preferred_element_type=F32)
        ou_ref[...] = lax.dot_general(wu_ref[...], hb, (((1,), (1,)), ((), ())),
                                      preferred_element_type=F32).astype(ou_ref.dtype)

    o_ref[...] = jnp.dot(h_scr[...], w_ref[...], preferred_element_type=F32).astype(o_ref.dtype)


def in_proj(x, g, shift, scale, w_main, w_gate, w_s5_t):
    t, d = x.shape
    n = w_main.shape[1]
    nu = w_s5_t.shape[0]
    tm = min(1024, t)
    tn = 1024
    vec = pl.BlockSpec((1, d), lambda i, j: (0, 0))
    return pl.pallas_call(
        _inproj_kernel,
        out_shape=(jax.ShapeDtypeStruct((t, n), BF16), jax.ShapeDtypeStruct((t, LANES), F32),
                   jax.ShapeDtypeStruct((nu, t), BF16)),
        grid=(t // tm, n // tn),
        in_specs=[pl.BlockSpec((tm, d), lambda i, j: (i, 0)), vec, vec, vec,
                  pl.BlockSpec((d, tn), lambda i, j: (0, j)),
                  pl.BlockSpec((d, LANES), lambda i, j: (0, 0)),
                  pl.BlockSpec((nu, d), lambda i, j: (0, 0))],
        out_specs=(pl.BlockSpec((tm, tn), lambda i, j: (i, j)),
                   pl.BlockSpec((tm, LANES), lambda i, j: (i, 0)),
                   pl.BlockSpec((nu, tm), lambda i, j: (0, i))),
        scratch_shapes=[pltpu.VMEM((tm, d), BF16)],
        compiler_params=_params(("parallel", "arbitrary")),
        name="in_proj",
    )(x, g, shift, scale, w_main, w_gate, w_s5_t)


def _sb_kernel(q_ref, k_ref, v_ref, o_ref, acc_ref, run_ref):
    blk = SB_BLOCK
    hd = SB_HEAD_DIM
    i = pl.program_id(1)
    scale = hd ** -0.5
    row = lax.broadcasted_iota(jnp.int32, (blk, blk), 0)
    col = lax.broadcasted_iota(jnp.int32, (blk, blk), 1)
    after = jnp.where(row > col, 1.0, 0.0).astype(BF16)
    causal = col < row
    acc_ref[...] = jnp.zeros(acc_ref.shape, F32)
    run_ref[...] = jnp.zeros(run_ref.shape, F32)

    def step(kb, diagonal):
        start = pl.multiple_of(kb * blk, blk)
        worst = None
        for h in range(SB_HEADS_PER_STEP):
            cs = slice(h * hd, (h + 1) * hd)
            z = lax.dot_general(q_ref[:, cs], k_ref[pl.ds(start, blk), cs], (((1,), (1,)), ((), ())),
                                preferred_element_type=F32) * scale
            softplus = jnp.maximum(z, 0.0) + jnp.log(1.0 + jnp.exp(-jnp.abs(z)))
            log_keep = -softplus
            log_beta = z - softplus
            if diagonal:
                log_keep = jnp.where(causal, log_keep, 0.0)
            hi = log_keep.astype(BF16)
            lo = (log_keep - hi.astype(F32)).astype(BF16)
            later = (jnp.dot(hi, after, preferred_element_type=F32)
                     + jnp.dot(lo, after, preferred_element_type=F32))
            run = run_ref[h]
            w = jnp.exp(log_beta + later + run)
            if diagonal:
                w = jnp.where(causal, w, 0.0)
            acc_ref[h] += jnp.dot(w.astype(BF16), v_ref[pl.ds(start, blk), cs], preferred_element_type=F32)
            run = run + jnp.sum(log_keep, axis=1, keepdims=True)
            run_ref[h] = run
            top = jnp.max(run)
            worst = top if worst is None else jnp.maximum(worst, top)
        return worst

    worst = step(i, True)

    def cond(carry):
        kb, worst = carry
        return jnp.logical_and(kb >= 0, worst > SB_LOG_CUTOFF)

    def body(carry):
        kb, _ = carry
        return kb - 1, step(kb, False)

    lax.while_loop(cond, body, (i - 1, worst))
    for h in range(SB_HEADS_PER_STEP):
        o_ref[:, h * hd:(h + 1) * hd] = acc_ref[h].astype(o_ref.dtype)


def sb_attention(proj):
    t = proj.shape[0]
    blk = SB_BLOCK
    wide = SB_HEADS_PER_STEP * SB_HEAD_DIM
    qb, kb, vb = COL_SB_Q // wide, COL_SB_K // wide, COL_SB_V // wide
    return pl.pallas_call(
        _sb_kernel,
        out_shape=jax.ShapeDtypeStruct((t, SB_WIDTH), BF16),
        grid=(SB_HEADS // SB_HEADS_PER_STEP, t // blk),
        in_specs=[pl.BlockSpec((blk, wide), lambda h, i: (i, qb + h)),
                  pl.BlockSpec((t, wide), lambda h, i: (0, kb + h)),
                  pl.BlockSpec((t, wide), lambda h, i: (0, vb + h))],
        out_specs=pl.BlockSpec((blk, wide), lambda h, i: (i, h)),
        scratch_shapes=[pltpu.VMEM((SB_HEADS_PER_STEP, blk, SB_HEAD_DIM), F32),
                        pltpu.VMEM((SB_HEADS_PER_STEP, blk, 1), F32)],
        compiler_params=_params(("parallel", "arbitrary")),
        name="sb_attention",
    )(proj, proj, proj)


def s5_weights(lam_re, lam_im, log_dt, b_re, b_im, c_re, c_im, d_skip):
    lc, g, p, hh = S5_CHUNK, S5_GROUPS, S5_STATE, S5_GROUP
    dt = jnp.exp(log_dt)[:, None]
    zr, zi = lam_re * dt, lam_im * dt
    e1 = jnp.exp(zr)
    lbr, lbi = e1 * jnp.cos(zi), e1 * jnp.sin(zi)
    den = lam_re * lam_re + lam_im * lam_im
    fr = ((lbr - 1.0) * lam_re + lbi * lam_im) / den
    fi = (lbi * lam_re - (lbr - 1.0) * lam_im) / den
    bbr = fr[:, :, None] * b_re - fi[:, :, None] * b_im
    bbi = fr[:, :, None] * b_im + fi[:, :, None] * b_re
    tau = jnp.arange(LANES, dtype=F32)
    used = tau < lc

    def c_times_power(shift):
        e = (tau + shift) * used
        mag = jnp.exp(zr[:, :, None, None] * e)
        pr = mag * jnp.cos(zi[:, :, None, None] * e)
        pi = mag * jnp.sin(zi[:, :, None, None] * e)
        cr = c_re.transpose(0, 2, 1)[:, :, :, None]
        ci = c_im.transpose(0, 2, 1)[:, :, :, None]
        return jnp.where(used, cr * pr - ci * pi, 0.0), jnp.where(used, cr * pi + ci * pr, 0.0)

    cp0r, cp0i = c_times_power(0.0)
    k_rows = (jnp.einsum('gpk,gpn->gkn', bbr, cp0r.reshape(g, p, hh * LANES), precision=HIGHEST)
              - jnp.einsum('gpk,gpn->gkn', bbi, cp0i.reshape(g, p, hh * LANES), precision=HIGHEST))
    cp1r, cp1i = c_times_power(1.0)
    w_out = jnp.concatenate([cp1r, -cp1i], axis=1).reshape(g, 2 * p, hh * LANES).astype(BF16)
    rev = (lc - 1.0) - jnp.arange(lc, dtype=F32)[None, :, None]
    rmag = jnp.exp(zr[:, None, :] * rev)
    rev_r, rev_i = rmag * jnp.cos(zi[:, None, :] * rev), rmag * jnp.sin(zi[:, None, :] * rev)
    bt_r, bt_i = bbr.transpose(0, 2, 1), bbi.transpose(0, 2, 1)
    w_state = (jnp.concatenate([rev_r, rev_r], -1)[:, None] * jnp.concatenate([bt_r, bt_i], -1)[:, :, None]
               + jnp.concatenate([-rev_i, rev_i], -1)[:, None] * jnp.concatenate([bt_i, bt_r], -1)[:, :, None]
               ).reshape(g, hh * lc, 2 * p).astype(BF16)
    full = jnp.exp(zr * lc)
    a_chunk = jnp.stack([full * jnp.cos(zi * lc), full * jnp.sin(zi * lc)], axis=1)
    d_row = jnp.broadcast_to(d_skip.reshape(g, 1, hh, 1), (g, 1, hh, LANES)).reshape(g, 1, hh * LANES)
    return k_rows, w_state, w_out, a_chunk, d_row


def _s5_kernel(u_ref, k_ref, ws_ref, wo_ref, a_ref, d_ref, o_ref, w_scr, sr_scr, si_scr, xr_scr, xi_scr):
    p = S5_STATE
    lc = S5_CHUNK
    hh = S5_GROUP
    nc = u_ref.shape[1]
    for hp in range(hh):
        rows = jnp.broadcast_to(k_ref[hp:hp + 1, :], (lc, hh * LANES))
        for h in range(hh):
            cs = slice(h * LANES, (h + 1) * LANES)
            w_scr[hp * lc:(hp + 1) * lc, cs] = pltpu.roll(rows[:, cs], 0, 1, stride=1, stride_axis=0).astype(BF16)

    uf = jnp.concatenate([u_ref[h].astype(F32) for h in range(hh)], axis=1)
    u = uf.astype(BF16)
    s = jnp.dot(u, ws_ref[...], preferred_element_type=F32)
    sr_scr[...] = s[:, :p]
    si_scr[...] = s[:, p:]
    ar = a_ref[0:1, :]
    ai = a_ref[1:2, :]

    def body(c, carry):
        xr, xi = carry
        xr_scr[pl.ds(c, 1), :] = xr
        xi_scr[pl.ds(c, 1), :] = xi
        sr = sr_scr[pl.ds(c, 1), :]
        si = si_scr[pl.ds(c, 1), :]
        return ar * xr - ai * xi + sr, ar * xi + ai * xr + si

    zero = jnp.zeros((1, p), F32)
    lax.fori_loop(0, nc, body, (zero, zero), unroll=8)
    y = jnp.dot(u, w_scr[...], preferred_element_type=F32)
    y = y + jnp.dot(xr_scr[...].astype(BF16), wo_ref[0:p, :], preferred_element_type=F32)
    y = y + jnp.dot(xi_scr[...].astype(BF16), wo_ref[p:2 * p, :], preferred_element_type=F32)
    for h in range(hh):
        cs = slice(h * LANES, h * LANES + lc)
        o_ref[h] = jax.nn.gelu(y[:, cs] + d_ref[:, cs] * u_ref[h].astype(F32)).astype(o_ref.dtype)


def s5_scan(u_t, k_rows, w_state, w_out, a_chunk, d_row):
    width, t = u_t.shape
    g, hh, lc, p = S5_GROUPS, S5_GROUP, S5_CHUNK, S5_STATE
    nc = t // lc
    grp = lambda n0, n1: pl.BlockSpec((None, n0, n1), lambda i: (i, 0, 0))
    chan = pl.BlockSpec((None, hh, nc, lc), lambda i: (i, 0, 0, 0))
    out = pl.pallas_call(
        _s5_kernel,
        out_shape=jax.ShapeDtypeStruct((g, hh, nc, lc), BF16),
        grid=(g,),
        in_specs=[chan, grp(hh, hh * LANES), grp(hh * lc, 2 * p), grp(2 * p, hh * LANES),
                  grp(2, p), grp(1, hh * LANES)],
        out_specs=chan,
        scratch_shapes=[pltpu.VMEM((hh * lc, hh * LANES), BF16)] + [pltpu.VMEM((nc, p), F32)] * 4,
        compiler_params=_params(("parallel",)),
        name="s5_scan",
    )(u_t.reshape(g, hh, nc, lc), k_rows, w_state, w_out, a_chunk, d_row)
    return out.reshape(width, t)


def _glu_kernel(y_ref, w_ref, o_ref):
    y = y_ref[...]
    a = jnp.dot(w_ref[...], y, preferred_element_type=F32)
    o_ref[...] = (y.astype(F32) * jax.nn.sigmoid(a)).T.astype(o_ref.dtype)


def s5_glu(y_t, w_glu_t):
    n, t = y_t.shape
    tn = min(1024, t)
    return pl.pallas_call(
        _glu_kernel,
        out_shape=jax.ShapeDtypeStruct((t, n), BF16),
        grid=(t // tn,),
        in_specs=[pl.BlockSpec((n, tn), lambda i: (0, i)),
                  pl.BlockSpec((n, n), lambda i: (0, 0))],
        out_specs=pl.BlockSpec((tn, n), lambda i: (i, 0)),
        compiler_params=_params(("parallel",)),
        name="s5_glu",
    )(y_t, w_glu_t)


ML_EXT = ML_HEAD_DIM + LANES


def _mlstm_kernel(q_ref, k_ref, v_ref, og_ref, g_ref, cw_ref, bias_ref, o_ref,
                  qbuf, kbuf, st_scr, m_scr):
    L = ML_CHUNK
    dh = ML_HEAD_DIM
    halo = SUBLANES
    c = pl.program_id(0)

    @pl.when(c == 0)
    def _():
        qbuf[0:halo, :] = jnp.zeros((halo, ML_WIDTH), F32)
        kbuf[0:halo, :] = jnp.zeros((halo, ML_WIDTH), F32)
        st_scr[...] = jnp.zeros(st_scr.shape, F32)
        m_scr[...] = jnp.zeros(m_scr.shape, F32)

    qbuf[halo:halo + L, :] = q_ref[...].astype(F32)
    kbuf[halo:halo + L, :] = k_ref[...].astype(F32)

    gp = g_ref[...] + bias_ref[...]
    lane = lax.broadcasted_iota(jnp.int32, gp.shape, 1)
    gl = jnp.where(lane < ML_HEADS, gp, _log_sigmoid(gp))
    r_i = lax.broadcasted_iota(jnp.int32, (L, L), 0)
    c_i = lax.broadcasted_iota(jnp.int32, (L, L), 1)
    tril = r_i >= c_i
    csum = jnp.dot(jnp.where(tril, 1.0, 0.0), gl, precision=HIGHEST,
                   preferred_element_type=F32)
    gl_t = gl.T
    csum_t = csum.T
    ones_col = (lax.broadcasted_iota(jnp.int32, (L, LANES), 1) == 0).astype(F32)

    for h in range(ML_HEADS):
        cs = slice(h * dh, (h + 1) * dh)

        def conv(buf, part):
            acc = None
            for j in range(ML_CONV):
                off = halo - (ML_CONV - 1) + j
                term = buf[off:off + L, cs] * cw_ref[j:j + 1, part * ML_WIDTH + h * dh:part * ML_WIDTH + (h + 1) * dh]
                acc = term if acc is None else acc + term
            return acc * jax.nn.sigmoid(acc)

        qh = conv(qbuf, 0)
        kh = conv(kbuf, 1) * (dh ** -0.5)
        vh = v_ref[:, cs]
        v_ext = jnp.concatenate([vh.astype(F32), ones_col], axis=1)

        li_col = gl[:, h:h + 1]
        b_col = csum[:, ML_HEADS + h:ML_HEADS + h + 1]
        li_row = gl_t[h:h + 1, :]
        b_row = csum_t[ML_HEADS + h:ML_HEADS + h + 1, :]
        b_tot = b_row[:, L - 1:L]
        m0 = m_scr[h:h + 1, 0:1]

        a_col = b_tot - b_col + li_col
        m_loc = jnp.max(a_col, axis=0, keepdims=True)
        wa = jnp.exp(a_col - m_loc)

        qb = qh.astype(BF16)
        qk = lax.dot_general(qb, kh.astype(BF16), (((1,), (1,)), ((), ())),
                             preferred_element_type=F32)
        log_d = jnp.where(tril, b_col - b_row + li_row, -jnp.inf)
        inter_log = b_col + m0
        m_t = jnp.maximum(inter_log, jnp.max(log_d, axis=1, keepdims=True))
        s_mat = jnp.exp(log_d - m_t) * qk
        w_inter = jnp.exp(inter_log - m_t)
        state = st_scr[h]
        nd = (jnp.dot(s_mat.astype(BF16), v_ext.astype(BF16), preferred_element_type=F32)
              + w_inter * jnp.dot(qb, state.astype(BF16), preferred_element_type=F32))
        num = nd[:, :dh]
        den = nd[:, dh:dh + 1]
        hval = num / jnp.maximum(jnp.abs(den), jnp.exp(-m_t))
        gate = jax.nn.sigmoid(og_ref[:, cs].astype(F32))
        o_ref[:, cs] = (hval * gate).astype(o_ref.dtype)

        m_new = jnp.maximum(b_tot + m0, m_loc)
        s_old = jnp.exp(b_tot + m0 - m_new)
        s_new = jnp.exp(m_loc - m_new)
        upd = jnp.dot(kh.T.astype(BF16), (wa * v_ext).astype(BF16), preferred_element_type=F32)
        st_scr[h] = s_old * state + s_new * upd
        m_scr[h:h + 1, :] = jnp.broadcast_to(m_new, (1, LANES))

    qbuf[0:halo, :] = qbuf[L:L + halo, :]
    kbuf[0:halo, :] = kbuf[L:L + halo, :]


def mlstm(proj, gates, conv_w, i_bias, f_bias):
    t = proj.shape[0]
    L = ML_CHUNK
    wb = ML_WIDTH
    bias = jnp.zeros((1, LANES), F32).at[0, :ML_HEADS].set(i_bias).at[0, ML_HEADS:2 * ML_HEADS].set(f_bias)
    col = lambda off: pl.BlockSpec((L, wb), lambda c, off=off: (c, off // wb))
    return pl.pallas_call(
        _mlstm_kernel,
        out_shape=jax.ShapeDtypeStruct((t, wb), BF16),
        grid=(t // L,),
        in_specs=[col(COL_ML_Q), col(COL_ML_K), col(COL_ML_V), col(COL_ML_O),
                  pl.BlockSpec((L, LANES), lambda c: (c, 0)),
                  pl.BlockSpec((ML_CONV, 2 * wb), lambda c: (0, 0)),
                  pl.BlockSpec((1, LANES), lambda c: (0, 0))],
        out_specs=pl.BlockSpec((L, wb), lambda c: (c, 0)),
        scratch_shapes=[pltpu.VMEM((L + SUBLANES, wb), F32), pltpu.VMEM((L + SUBLANES, wb), F32),
                        pltpu.VMEM((ML_HEADS, ML_HEAD_DIM, ML_EXT), F32),
                        pltpu.VMEM((SUBLANES, LANES), F32)],
        compiler_params=_params(("arbitrary",)),
        name="mlstm",
    )(proj, proj, proj, proj, gates, conv_w, bias)


def _merge_kernel(ysb_ref, ys5_ref, yml_ref, psb_ref, ps5_ref, pml_ref,
                  gsb_ref, gs5_ref, gml_ref, o_ref):
    def branch(y_ref, p_ref, g_ref):
        return (jax.nn.sigmoid(g_ref[...].astype(F32))
                * jnp.dot(y_ref[...], p_ref[...], preferred_element_type=F32))

    o_ref[...] = (branch(ysb_ref, psb_ref, gsb_ref) + branch(ys5_ref, ps5_ref, gs5_ref)
                  + branch(yml_ref, pml_ref, gml_ref)).astype(o_ref.dtype)


def merge(y_sb, y_s5, y_ml, p_sb, p_s5, p_ml, proj):
    t, k = y_sb.shape
    n = p_sb.shape[1]
    tm, tn = min(1024, t), 1024
    ysp = pl.BlockSpec((tm, k), lambda i, j: (i, 0))
    psp = pl.BlockSpec((k, tn), lambda i, j: (0, j))
    gsp = lambda off: pl.BlockSpec((tm, tn), lambda i, j, off=off: (i, off // tn + j))
    return pl.pallas_call(
        _merge_kernel,
        out_shape=jax.ShapeDtypeStruct((t, n), BF16),
        grid=(t // tm, n // tn),
        in_specs=[ysp, ysp, ysp, psp, psp, psp, gsp(COL_G_SB), gsp(COL_G_S5), gsp(COL_G_ML)],
        out_specs=pl.BlockSpec((tm, tn), lambda i, j: (i, j)),
        compiler_params=_params(("parallel", "parallel")),
        name="merge",
    )(y_sb, y_s5, y_ml, p_sb, p_s5, p_ml, proj, proj, proj)


def _route(logits):
    lane = lax.broadcasted_iota(jnp.int32, logits.shape, 1)
    neg = -jnp.inf
    big = jnp.int32(LANES)
    is_g = jnp.logical_and(lane >= N_EXPERTS, lane < N_EXPERTS + N_GROUPS)
    gl = jnp.where(is_g, logits, neg)
    gmax = jnp.max(gl, axis=1, keepdims=True)
    g_idx = jnp.min(jnp.where(gl == gmax, lane, big), axis=1, keepdims=True) - N_EXPERTS
    g_w = 1.0 / jnp.sum(jnp.where(is_g, jnp.exp(gl - gmax), 0.0), axis=1, keepdims=True)
    in_grp = jnp.logical_and(lane >= g_idx * EXPERTS_PER_GROUP, lane < (g_idx + 1) * EXPERTS_PER_GROUP)
    e1 = jnp.where(in_grp, logits, neg)
    v1 = jnp.max(e1, axis=1, keepdims=True)
    i1 = jnp.min(jnp.where(e1 == v1, lane, big), axis=1, keepdims=True)
    e2 = jnp.where(lane == i1, neg, e1)
    v2 = jnp.max(e2, axis=1, keepdims=True)
    i2 = jnp.min(jnp.where(e2 == v2, lane, big), axis=1, keepdims=True)
    ex = jnp.exp(v2 - v1)
    return i1, i2, g_w / (1.0 + ex), g_w * ex / (1.0 + ex)


META_E1, META_E2, META_W1, META_W2, META_POS1, META_POS2 = range(6)


def _outproj_kernel(m_ref, w_ref, x_ref, g1_ref, ng_ref, sh_ref, sc_ref, wr_ref, br_ref,
                    xo_ref, h_ref, meta_ref, cnt_ref, carry):
    @pl.when(pl.program_id(0) == 0)
    def _():
        carry[...] = jnp.zeros(carry.shape, F32)

    y = jnp.dot(m_ref[...], w_ref[...], preferred_element_type=F32)
    x = x_ref[...] + g1_ref[...] * y
    xo_ref[...] = x
    h = _norm_mod(x, ng_ref[...], sh_ref[...], sc_ref[...])
    h_ref[...] = h
    logits = jnp.dot(h.astype(BF16), wr_ref[...], preferred_element_type=F32) + br_ref[...]
    i1, i2, w1, w2 = _route(logits)

    tm = logits.shape[0]
    lane = lax.broadcasted_iota(jnp.int32, logits.shape, 1)
    picked = jnp.logical_or(lane == i1, lane == i2)
    r_i = lax.broadcasted_iota(jnp.int32, (tm, tm), 0)
    c_i = lax.broadcasted_iota(jnp.int32, (tm, tm), 1)
    earlier = jnp.where(r_i > c_i, 1.0, 0.0).astype(BF16)
    before = jnp.dot(earlier, jnp.where(picked, 1.0, 0.0).astype(BF16), preferred_element_type=F32)
    before = before + carry[0:1, :]
    pos1 = jnp.sum(jnp.where(lane == i1, before, 0.0), axis=1, keepdims=True)
    pos2 = jnp.sum(jnp.where(lane == i2, before, 0.0), axis=1, keepdims=True)
    total = carry[0:1, :] + jnp.sum(jnp.where(picked, 1.0, 0.0), axis=0, keepdims=True)
    carry[...] = jnp.broadcast_to(total, carry.shape)
    cnt_ref[...] = jnp.broadcast_to(total, cnt_ref.shape)

    meta = jnp.zeros(logits.shape, F32)
    for k, val in ((META_E1, i1.astype(F32)), (META_E2, i2.astype(F32)), (META_W1, w1), (META_W2, w2),
                   (META_POS1, pos1), (META_POS2, pos2)):
        meta = jnp.where(lane == k, val, meta)
    meta_ref[...] = meta


def out_proj(merged, w_out, x, gate1, norm_g, shift, scale, w_route, b_route):
    t, d = x.shape
    tm = min(256, t)
    row = pl.BlockSpec((tm, d), lambda i: (i, 0))
    vec = pl.BlockSpec((1, d), lambda i: (0, 0))
    return pl.pallas_call(
        _outproj_kernel,
        out_shape=(jax.ShapeDtypeStruct((t, d), F32), jax.ShapeDtypeStruct((t, d), F32),
                   jax.ShapeDtypeStruct((t, LANES), F32), jax.ShapeDtypeStruct((SUBLANES, LANES), F32)),
        grid=(t // tm,),
        in_specs=[row, pl.BlockSpec((d, d), lambda i: (0, 0)), row, vec, vec, vec, vec,
                  pl.BlockSpec((d, LANES), lambda i: (0, 0)), pl.BlockSpec((1, LANES), lambda i: (0, 0))],
        out_specs=(row, row, pl.BlockSpec((tm, LANES), lambda i: (i, 0)),
                   pl.BlockSpec((SUBLANES, LANES), lambda i: (0, 0))),
        scratch_shapes=[pltpu.VMEM((SUBLANES, LANES), F32)],
        compiler_params=_params(("arbitrary",)),
        name="out_proj",
    )(merged, w_out, x, gate1, norm_g, shift, scale, w_route, b_route)


def moe_plan(meta, counts, n_tiles):
    tile = MOE_TILE
    as_int = lambda k: meta[:, k].astype(jnp.int32)
    cnt = counts[0, :N_EXPERTS].astype(jnp.int32)
    padded = ((cnt + tile - 1) // tile) * tile
    seg_end = jnp.cumsum(padded)
    seg_off = seg_end - padded
    n_used = (seg_end[-1] // tile).reshape(1)
    starts = jnp.arange(n_tiles, dtype=jnp.int32) * tile
    tile_expert = jnp.sum((starts[:, None] >= seg_end[None, :]).astype(jnp.int32), axis=1)
    tile_expert = jnp.minimum(tile_expert, N_EXPERTS - 1)
    picks = (as_int(META_E1), as_int(META_E2), as_int(META_POS1), as_int(META_POS2), seg_off)
    return picks, tile_expert, n_used


def _row_copy(src, src_row, dst, dst_row, sem):
    return pltpu.make_async_copy(src.at[pl.ds(src_row, 1)], dst.at[pl.ds(dst_row, 1)], sem)


def _experts_kernel(e1_ref, e2_ref, p1_ref, p2_ref, off_ref, te_ref, nu_ref,
                    h_ref, w1_ref, w3_ref, w2_ref, y_ref, token_of, buf, sems, w1b, w3b, w2b):
    j = pl.program_id(0)
    tile = MOE_TILE
    n_used = nu_ref[0]

    @pl.when(j == 0)
    def _():
        def clear(s, carry):
            token_of[s] = 0
            return carry

        def fill(t, carry):
            token_of[off_ref[e1_ref[t]] + p1_ref[t]] = t
            token_of[off_ref[e2_ref[t]] + p2_ref[t]] = t
            return carry

        lax.fori_loop(0, token_of.shape[0], clear, 0, unroll=8)
        lax.fori_loop(0, e1_ref.shape[0], fill, 0, unroll=4)

        def first(r, carry):
            _row_copy(h_ref, token_of[r], buf.at[0], r, sems.at[0]).start()
            return carry

        lax.fori_loop(0, tile, first, 0, unroll=4)

    def tile_done(slot):
        return pltpu.make_async_copy(h_ref.at[pl.ds(0, tile)], buf.at[slot], sems.at[slot])

    def run_tile(slot):
        base = jnp.minimum(j + 1, n_used - 1) * tile
        for r in range(tile):
            _row_copy(h_ref, token_of[base + r], buf.at[1 - slot], r, sems.at[1 - slot]).start()
        tile_done(slot).wait()
        x = buf[slot].astype(BF16)
        a = jnp.dot(x, w1b[...], preferred_element_type=F32)
        b = jnp.dot(x, w3b[...], preferred_element_type=F32)
        hid = (a * jax.nn.sigmoid(a) * b).astype(BF16)
        y_ref[...] = jnp.dot(hid, w2b[...], preferred_element_type=F32)

        @pl.when(j == n_used - 1)
        def _():
            tile_done(1 - slot).wait()

    @pl.when(j < n_used)
    def _():
        @pl.when(jnp.logical_or(j == 0, te_ref[j] != te_ref[jnp.maximum(j - 1, 0)]))
        def _():
            w1b[...] = w1_ref[...].astype(BF16)
            w3b[...] = w3_ref[...].astype(BF16)
            w2b[...] = w2_ref[...].astype(BF16)

        for slot in range(2):
            pl.when(j % 2 == slot)(functools.partial(run_tile, slot))

    @pl.when(j >= n_used)
    def _():
        y_ref[...] = jnp.zeros(y_ref.shape, y_ref.dtype)


def moe_experts(h, picks, tile_expert, n_used, w1, w3, w2, n_slots):
    t, d = h.shape
    ne, _, ff = w1.shape
    tile = MOE_TILE
    n_pref = len(picks) + 2

    def wspec(r, c):
        def index(j, *pref):
            te, nu = pref[-2], pref[-1]
            return (te[jnp.minimum(j, nu[0] - 1)], 0, 0)
        return pl.BlockSpec((None, r, c), index)

    return pl.pallas_call(
        _experts_kernel,
        out_shape=jax.ShapeDtypeStruct((n_slots, d), F32),
        grid_spec=pltpu.PrefetchScalarGridSpec(
            num_scalar_prefetch=n_pref,
            grid=(n_slots // tile,),
            in_specs=[pl.BlockSpec(memory_space=pl.ANY), wspec(d, ff), wspec(d, ff), wspec(ff, d)],
            out_specs=pl.BlockSpec((tile, d), lambda j, *pref: (j, 0)),
            scratch_shapes=[pltpu.SMEM((n_slots,), jnp.int32), pltpu.VMEM((2, tile, d), F32),
                            pltpu.SemaphoreType.DMA((2,)),
                            pltpu.VMEM((d, ff), BF16), pltpu.VMEM((d, ff), BF16), pltpu.VMEM((ff, d), BF16)]),
        compiler_params=_params(("arbitrary",)),
        name="moe_experts",
    )(*picks, tile_expert, n_used, h, w1, w3, w2)


def _combine_kernel(e1_ref, e2_ref, p1_ref, p2_ref, off_ref, y_ref, x_ref, meta_ref, g2_ref, fg_ref,
                    o_ref, buf, sems, *, final_norm):
    i = pl.program_id(0)
    tm = x_ref.shape[0]

    def gather(tile_idx, slot):
        base = tile_idx * tm

        def issue(r, carry):
            t = base + r
            _row_copy(y_ref, off_ref[e1_ref[t]] + p1_ref[t], buf.at[slot, 0], r, sems.at[slot]).start()
            _row_copy(y_ref, off_ref[e2_ref[t]] + p2_ref[t], buf.at[slot, 1], r, sems.at[slot]).start()
            return carry

        lax.fori_loop(0, tm, issue, 0, unroll=4)

    @pl.when(i == 0)
    def _():
        gather(0, 0)

    @pl.when(i + 1 < pl.num_programs(0))
    def _():
        gather(i + 1, (i + 1) % 2)

    slot = i % 2
    for k in range(2):
        pltpu.make_async_copy(y_ref.at[pl.ds(0, tm)], buf.at[slot, k], sems.at[slot]).wait()

    meta = meta_ref[...]
    mix = meta[:, META_W1:META_W1 + 1] * buf[slot, 0] + meta[:, META_W2:META_W2 + 1] * buf[slot, 1]
    x = x_ref[...] + g2_ref[...] * mix
    if final_norm:
        ms = jnp.mean(x * x, axis=-1, keepdims=True)
        x = x * lax.rsqrt(ms + EPS) * fg_ref[...]
    o_ref[...] = x


def moe_combine(ys, picks, x, meta, gate2, final_g, final_norm):
    t, d = x.shape
    tm = min(256, t)
    row = pl.BlockSpec((tm, d), lambda i, *pref: (i, 0))
    vec = pl.BlockSpec((1, d), lambda i, *pref: (0, 0))
    return pl.pallas_call(
        functools.partial(_combine_kernel, final_norm=final_norm),
        out_shape=jax.ShapeDtypeStruct((t, d), F32),
        grid_spec=pltpu.PrefetchScalarGridSpec(
            num_scalar_prefetch=len(picks),
            grid=(t // tm,),
            in_specs=[pl.BlockSpec(memory_space=pl.ANY), row,
                      pl.BlockSpec((tm, LANES), lambda i, *pref: (i, 0)), vec, vec],
            out_specs=row,
            scratch_shapes=[pltpu.VMEM((2, 2, tm, d), F32), pltpu.SemaphoreType.DMA((2,))]),
        compiler_params=_params(("arbitrary",)),
        name="moe_combine",
    )(*picks, ys, x, meta, gate2, final_g)


def kernel(x, c, norm_mix_g, norm_moe_g, final_g, w_ada, b_ada, w_in, ml_conv, ml_i_bias, ml_f_bias,
           s5_lam_re, s5_lam_im, s5_log_dt, s5_b_re, s5_b_im, s5_c_re, s5_c_im, s5_d, s5_w_glu,
           p_sb, p_s5, p_ml, w_out, moe_w_group, moe_b_group, moe_w_router, moe_b_router,
           moe_w1, moe_w3, moe_w2):
    bsz, seq, d = x.shape
    assert bsz == 1 and d == D_MODEL
    assert seq % max(ML_CHUNK, SB_BLOCK, S5_CHUNK * SUBLANES) == 0
    depth = w_in.shape[0]
    xt = x.reshape(seq, d)
    mod = ada_mod(c, w_ada, b_ada)
    for l in range(depth):
        shift1, scale1, gate1, shift2, scale2, gate2 = [mod[l, :, i * d:(i + 1) * d] for i in range(6)]
        w_l = w_in[l]
        w_main = jnp.concatenate([w_l[:, :WIN_S5_U], w_l[:, WIN_ML_Q:WIN_GATES], w_l[:, WIN_G_SB:]],
                                 axis=1).astype(BF16)
        w_gate = jnp.pad(w_l[:, WIN_GATES:WIN_G_SB], ((0, 0), (0, LANES - N_GATE_COLS))).astype(BF16)
        w_s5_t = w_l[:, WIN_S5_U:WIN_ML_Q].T.astype(BF16)
        proj, gates, u_t = in_proj(xt, norm_mix_g[l].reshape(1, d), shift1, scale1, w_main, w_gate, w_s5_t)

        y_sb = sb_attention(proj)

        y_t = s5_scan(u_t, *s5_weights(s5_lam_re[l], s5_lam_im[l], s5_log_dt[l], s5_b_re[l], s5_b_im[l],
                                       s5_c_re[l], s5_c_im[l], s5_d[l]))
        y_s5 = s5_glu(y_t, s5_w_glu[l].T.astype(BF16))

        y_ml = mlstm(proj, gates, ml_conv[l], ml_i_bias[l], ml_f_bias[l])

        merged = merge(y_sb, y_s5, y_ml, p_sb[l].astype(BF16), p_s5[l].astype(BF16), p_ml[l].astype(BF16), proj)

        w_route = jnp.pad(jnp.concatenate([moe_w_router[l], moe_w_group[l]], axis=1),
                          ((0, 0), (0, LANES - N_EXPERTS - N_GROUPS))).astype(BF16)
        b_route = jnp.pad(jnp.concatenate([moe_b_router[l], moe_b_group[l]]),
                          (0, LANES - N_EXPERTS - N_GROUPS)).reshape(1, LANES)
        xt, h2, meta, counts = out_proj(merged, w_out[l].astype(BF16), xt, gate1, norm_moe_g[l].reshape(1, d),
                                        shift2, scale2, w_route, b_route)
        n_slots = 2 * seq + N_EXPERTS * MOE_TILE
        picks, tile_expert, n_used = moe_plan(meta, counts, n_slots // MOE_TILE)
        ys = moe_experts(h2, picks, tile_expert, n_used, moe_w1[l], moe_w3[l], moe_w2[l], n_slots)
        xt = moe_combine(ys, picks, xt, meta, gate2, final_g.reshape(1, d), final_norm=(l == depth - 1))
    return xt.reshape(bsz, seq, d)
```

```python
import functools
import math

import jax
import jax.numpy as jnp
from jax import lax
from jax.experimental import pallas as pl
from jax.experimental.pallas import tpu as pltpu

F32 = jnp.float32
BF16 = jnp.bfloat16
HIGHEST = lax.Precision.HIGHEST

D_MODEL = 2048
DEPTH = 2
SB_HEADS = 8
SB_HEAD_DIM = 128
SB_WIDTH = SB_HEADS * SB_HEAD_DIM
S5_WIDTH = 1024
S5_GROUP = 16
S5_GROUPS = S5_WIDTH // S5_GROUP
S5_STATE = 64
ML_HEADS = 4
ML_HEAD_DIM = 256
ML_WIDTH = ML_HEADS * ML_HEAD_DIM
ML_CONV = 4
N_GROUPS = 4
EXPERTS_PER_GROUP = 8
N_EXPERTS = N_GROUPS * EXPERTS_PER_GROUP
EXPERT_FF = 256
EPS = 1e-6

LANES = 128
SUBLANES = 8
VMEM_LIMIT = 52 * 1024 * 1024

WIN_S5_U = 3072
WIN_ML_Q = 4096
WIN_GATES = 8192
N_GATE_COLS = 2 * ML_HEADS
WIN_G_SB = WIN_GATES + N_GATE_COLS
COL_SB_Q, COL_SB_K, COL_SB_V = 0, 1024, 2048
COL_ML_Q, COL_ML_K, COL_ML_V, COL_ML_O = 3072, 4096, 5120, 6144
COL_G_SB, COL_G_S5, COL_G_ML = 7168, 9216, 11264

S5_CHUNK = 64
ML_CHUNK = 256
SB_BLOCK = 256
SB_HEADS_PER_STEP = 2
MOE_TILE = 256
SB_LOG_CUTOFF = -105.0


def _params(sem, vmem=VMEM_LIMIT):
    return pltpu.CompilerParams(dimension_semantics=sem, vmem_limit_bytes=vmem)


def _log_sigmoid(z):
    return jnp.minimum(z, 0.0) - jnp.log1p(jnp.exp(-jnp.abs(z)))


def _ada_kernel(c_ref, w_ref, b_ref, o_ref):
    c = c_ref[...]
    ca = c * jax.nn.sigmoid(c)
    r = jnp.dot(ca, w_ref[...], precision=HIGHEST, preferred_element_type=F32)
    o_ref[...] = r[0:1, :] + b_ref[...]


def ada_mod(c, w_ada, b_ada):
    depth, d, n = w_ada.shape
    tn = 1024
    c8 = jnp.broadcast_to(c.reshape(1, d), (SUBLANES, d))
    return pl.pallas_call(
        _ada_kernel,
        out_shape=jax.ShapeDtypeStruct((depth, 1, n), F32),
        grid=(depth, n // tn),
        in_specs=[pl.BlockSpec((SUBLANES, d), lambda l, j: (0, 0)),
                  pl.BlockSpec((None, d, tn), lambda l, j: (l, 0, j)),
                  pl.BlockSpec((None, 1, tn), lambda l, j: (l, 0, j))],
        out_specs=pl.BlockSpec((None, 1, tn), lambda l, j: (l, 0, j)),
        compiler_params=_params(("parallel", "parallel")),
        name="ada_mod",
    )(c8, w_ada, b_ada.reshape(depth, 1, n))


def _norm_mod(x, g, shift, scale):
    ms = jnp.mean(x * x, axis=-1, keepdims=True)
    y = x * lax.rsqrt(ms + EPS) * g
    return y * (1.0 + scale) + shift


def _inproj_kernel(x_ref, g_ref, sh_ref, sc_ref, w_ref, wg_ref, wu_ref, o_ref, og_ref, ou_ref, h_scr):
    @pl.when(pl.program_id(1) == 0)
    def _():
        hb = _norm_mod(x_ref[...], g_ref[...], sh_ref[...], sc_ref[...]).astype(BF16)
        h_scr[...] = hb
        og_ref[...] = jnp.dot(hb, wg_ref[...], preferred_element_type=F32)
        ou_ref[...] = lax.dot_general(wu_ref[...], hb, (((1,), (1,)), ((), ())),
                                      preferred_element_type=F32).astype(ou_ref.dtype)

    o_ref[...] = jnp.dot(h_scr[...], w_ref[...], preferred_element_type=F32).astype(o_ref.dtype)


def in_proj(x, g, shift, scale, w_main, w_gate, w_s5_t):
    t, d = x.shape
    n = w_main.shape[1]
    nu = w_s5_t.shape[0]
    tm = min(1024, t)
    tn = 1024
    vec = pl.BlockSpec((1, d), lambda i, j: (0, 0))
    return pl.pallas_call(
        _inproj_kernel,
        out_shape=(jax.ShapeDtypeStruct((t, n), BF16), jax.ShapeDtypeStruct((t, LANES), F32),
                   jax.ShapeDtypeStruct((nu, t), BF16)),
        grid=(t // tm, n // tn),
        in_specs=[pl.BlockSpec((tm, d), lambda i, j: (i, 0)), vec, vec, vec,
                  pl.BlockSpec((d, tn), lambda i, j: (0, j)),
                  pl.BlockSpec((d, LANES), lambda i, j: (0, 0)),
                  pl.BlockSpec((nu, d), lambda i, j: (0, 0))],
        out_specs=(pl.BlockSpec((tm, tn), lambda i, j: (i, j)),
                   pl.BlockSpec((tm, LANES), lambda i, j: (i, 0)),
                   pl.BlockSpec((nu, tm), lambda i, j: (0, i))),
        scratch_shapes=[pltpu.VMEM((tm, d), BF16)],
        compiler_params=_params(("parallel", "arbitrary")),
        name="in_proj",
    )(x, g, shift, scale, w_main, w_gate, w_s5_t)


def _sb_kernel(q_ref, k_ref, v_ref, o_ref, acc_ref, run_ref):
    blk = SB_BLOCK
    hd = SB_HEAD_DIM
    i = pl.program_id(1)
    scale = hd ** -0.5
    row = lax.broadcasted_iota(jnp.int32, (blk, blk), 0)
    col = lax.broadcasted_iota(jnp.int32, (blk, blk), 1)
    after = jnp.where(row > col, 1.0, 0.0).astype(BF16)
    causal = col < row
    acc_ref[...] = jnp.zeros(acc_ref.shape, F32)
    run_ref[...] = jnp.zeros(run_ref.shape, F32)

    def step(kb, diagonal):
        start = pl.multiple_of(kb * blk, blk)
        worst = None
        for h in range(SB_HEADS_PER_STEP):
            cs = slice(h * hd, (h + 1) * hd)
            z = lax.dot_general(q_ref[:, cs], k_ref[pl.ds(start, blk), cs], (((1,), (1,)), ((), ())),
                                preferred_element_type=F32) * scale
            softplus = jnp.maximum(z, 0.0) + jnp.log(1.0 + jnp.exp(-jnp.abs(z)))
            log_keep = -softplus
            log_beta = z - softplus
            if diagonal:
                log_keep = jnp.where(causal, log_keep, 0.0)
            hi = log_keep.astype(BF16)
            lo = (log_keep - hi.astype(F32)).astype(BF16)
            later = (jnp.dot(hi, after, preferred_element_type=F32)
                     + jnp.dot(lo, after, preferred_element_type=F32))
            run = run_ref[h]
            w = jnp.exp(log_beta + later + run)
            if diagonal:
                w = jnp.where(causal, w, 0.0)
            acc_ref[h] += jnp.dot(w.astype(BF16), v_ref[pl.ds(start, blk), cs], preferred_element_type=F32)
            run = run + jnp.sum(log_keep, axis=1, keepdims=True)
            run_ref[h] = run
            top = jnp.max(run)
            worst = top if worst is None else jnp.maximum(worst, top)
        return worst

    worst = step(i, True)

    def cond(carry):
        kb, worst = carry
        return jnp.logical_and(kb >= 0, worst > SB_LOG_CUTOFF)

    def body(carry):
        kb, _ = carry
        return kb - 1, step(kb, False)

    lax.while_loop(cond, body, (i - 1, worst))
    for h in range(SB_HEADS_PER_STEP):
        o_ref[:, h * hd:(h + 1) * hd] = acc_ref[h].astype(o_ref.dtype)


def sb_attention(proj):
    t = proj.shape[0]
    blk = SB_BLOCK
    wide = SB_HEADS_PER_STEP * SB_HEAD_DIM
    qb, kb, vb = COL_SB_Q // wide, COL_SB_K // wide, COL_SB_V // wide
    return pl.pallas_call(
        _sb_kernel,
        out_shape=jax.ShapeDtypeStruct((t, SB_WIDTH), BF16),
        grid=(SB_HEADS // SB_HEADS_PER_STEP, t // blk),
        in_specs=[pl.BlockSpec((blk, wide), lambda h, i: (i, qb + h)),
                  pl.BlockSpec((t, wide), lambda h, i: (0, kb + h)),
                  pl.BlockSpec((t, wide), lambda h, i: (0, vb + h))],
        out_specs=pl.BlockSpec((blk, wide), lambda h, i: (i, h)),
        scratch_shapes=[pltpu.VMEM((SB_HEADS_PER_STEP, blk, SB_HEAD_DIM), F32),
                        pltpu.VMEM((SB_HEADS_PER_STEP, blk, 1), F32)],
        compiler_params=_params(("parallel", "arbitrary")),
        name="sb_attention",
    )(proj, proj, proj)


def s5_weights(lam_re, lam_im, log_dt, b_re, b_im, c_re, c_im, d_skip):
    lc, g, p, hh = S5_CHUNK, S5_GROUPS, S5_STATE, S5_GROUP
    dt = jnp.exp(log_dt)[:, None]
    zr, zi = lam_re * dt, lam_im * dt
    e1 = jnp.exp(zr)
    lbr, lbi = e1 * jnp.cos(zi), e1 * jnp.sin(zi)
    den = lam_re * lam_re + lam_im * lam_im
    fr = ((lbr - 1.0) * lam_re + lbi * lam_im) / den
    fi = (lbi * lam_re - (lbr - 1.0) * lam_im) / den
    bbr = fr[:, :, None] * b_re - fi[:, :, None] * b_im
    bbi = fr[:, :, None] * b_im + fi[:, :, None] * b_re
    tau = jnp.arange(LANES, dtype=F32)
    used = tau < lc

    def c_times_power(shift):
        e = (tau + shift) * used
        mag = jnp.exp(zr[:, :, None, None] * e)
        pr = mag * jnp.cos(zi[:, :, None, None] * e)
        pi = mag * jnp.sin(zi[:, :, None, None] * e)
        cr = c_re.transpose(0, 2, 1)[:, :, :, None]
        ci = c_im.transpose(0, 2, 1)[:, :, :, None]
        return jnp.where(used, cr * pr - ci * pi, 0.0), jnp.where(used, cr * pi + ci * pr, 0.0)

    cp0r, cp0i = c_times_power(0.0)
    k_rows = (jnp.einsum('gpk,gpn->gkn', bbr, cp0r.reshape(g, p, hh * LANES), precision=HIGHEST)
              - jnp.einsum('gpk,gpn->gkn', bbi, cp0i.reshape(g, p, hh * LANES), precision=HIGHEST))
    cp1r, cp1i = c_times_power(1.0)
    w_out = jnp.concatenate([cp1r, -cp1i], axis=1).reshape(g, 2 * p, hh * LANES).astype(BF16)
    rev = (lc - 1.0) - jnp.arange(lc, dtype=F32)[None, :, None]
    rmag = jnp.exp(zr[:, None, :] * rev)
    rev_r, rev_i = rmag * jnp.cos(zi[:, None, :] * rev), rmag * jnp.sin(zi[:, None, :] * rev)
    bt_r, bt_i = bbr.transpose(0, 2, 1), bbi.transpose(0, 2, 1)
    w_state = (jnp.concatenate([rev_r, rev_r], -1)[:, None] * jnp.concatenate([bt_r, bt_i], -1)[:, :, None]
               + jnp.concatenate([-rev_i, rev_i], -1)[:, None] * jnp.concatenate([bt_i, bt_r], -1)[:, :, None]
               ).reshape(g, hh * lc, 2 * p).astype(BF16)
    full = jnp.exp(zr * lc)
    a_chunk = jnp.stack([full * jnp.cos(zi * lc), full * jnp.sin(zi * lc)], axis=1)
    d_row = jnp.broadcast_to(d_skip.reshape(g, 1, hh, 1), (g, 1, hh, LANES)).reshape(g, 1, hh * LANES)
    return k_rows, w_state, w_out, a_chunk, d_row


def _s5_kernel(u_ref, k_ref, ws_ref, wo_ref, a_ref, d_ref, o_ref, w_scr, sr_scr, si_scr, xr_scr, xi_scr):
    p = S5_STATE
    lc = S5_CHUNK
    hh = S5_GROUP
    nc = u_ref.shape[1]
    for hp in range(hh):
        rows = jnp.broadcast_to(k_ref[hp:hp + 1, :], (lc, hh * LANES))
        for h in range(hh):
            cs = slice(h * LANES, (h + 1) * LANES)
            w_scr[hp * lc:(hp + 1) * lc, cs] = pltpu.roll(rows[:, cs], 0, 1, stride=1, stride_axis=0).astype(BF16)

    uf = jnp.concatenate([u_ref[h].astype(F32) for h in range(hh)], axis=1)
    u = uf.astype(BF16)
    s = jnp.dot(u, ws_ref[...], preferred_element_type=F32)
    sr_scr[...] = s[:, :p]
    si_scr[...] = s[:, p:]
    ar = a_ref[0:1, :]
    ai = a_ref[1:2, :]

    def body(c, carry):
        xr, xi = carry
        xr_scr[pl.ds(c, 1), :] = xr
        xi_scr[pl.ds(c, 1), :] = xi
        sr = sr_scr[pl.ds(c, 1), :]
        si = si_scr[pl.ds(c, 1), :]
        return ar * xr - ai * xi + sr, ar * xi + ai * xr + si

    zero = jnp.zeros((1, p), F32)
    lax.fori_loop(0, nc, body, (zero, zero), unroll=8)
    y = jnp.dot(u, w_scr[...], preferred_element_type=F32)
    y = y + jnp.dot(xr_scr[...].astype(BF16), wo_ref[0:p, :], preferred_element_type=F32)
    y = y + jnp.dot(xi_scr[...].astype(BF16), wo_ref[p:2 * p, :], preferred_element_type=F32)
    for h in range(hh):
        cs = slice(h * LANES, h * LANES + lc)
        o_ref[h] = jax.nn.gelu(y[:, cs] + d_ref[:, cs] * u_ref[h].astype(F32)).astype(o_ref.dtype)


def s5_scan(u_t, k_rows, w_state, w_out, a_chunk, d_row):
    width, t = u_t.shape
    g, hh, lc, p = S5_GROUPS, S5_GROUP, S5_CHUNK, S5_STATE
    nc = t // lc
    grp = lambda n0, n1: pl.BlockSpec((None, n0, n1), lambda i: (i, 0, 0))
    chan = pl.BlockSpec((None, hh, nc, lc), lambda i: (i, 0, 0, 0))
    out = pl.pallas_call(
        _s5_kernel,
        out_shape=jax.ShapeDtypeStruct((g, hh, nc, lc), BF16),
        grid=(g,),
        in_specs=[chan, grp(hh, hh * LANES), grp(hh * lc, 2 * p), grp(2 * p, hh * LANES),
                  grp(2, p), grp(1, hh * LANES)],
        out_specs=chan,
        scratch_shapes=[pltpu.VMEM((hh * lc, hh * LANES), BF16)] + [pltpu.VMEM((nc, p), F32)] * 4,
        compiler_params=_params(("parallel",)),
        name="s5_scan",
    )(u_t.reshape(g, hh, nc, lc), k_rows, w_state, w_out, a_chunk, d_row)
    return out.reshape(width, t)


def _glu_kernel(y_ref, w_ref, o_ref):
    y = y_ref[...]
    a = jnp.dot(w_ref[...], y, preferred_element_type=F32)
    o_ref[...] = (y.astype(F32) * jax.nn.sigmoid(a)).T.astype(o_ref.dtype)


def s5_glu(y_t, w_glu_t):
    n, t = y_t.shape
    tn = min(1024, t)
    return pl.pallas_call(
        _glu_kernel,
        out_shape=jax.ShapeDtypeStruct((t, n), BF16),
        grid=(t // tn,),
        in_specs=[pl.BlockSpec((n, tn), lambda i: (0, i)),
                  pl.BlockSpec((n, n), lambda i: (0, 0))],
        out_specs=pl.BlockSpec((tn, n), lambda i: (i, 0)),
        compiler_params=_params(("parallel",)),
        name="s5_glu",
    )(y_t, w_glu_t)


ML_EXT = ML_HEAD_DIM + LANES


def _mlstm_kernel(q_ref, k_ref, v_ref, og_ref, g_ref, cw_ref, bias_ref, o_ref,
                  qbuf, kbuf, st_scr, m_scr):
    L = ML_CHUNK
    dh = ML_HEAD_DIM
    halo = SUBLANES
    c = pl.program_id(0)

    @pl.when(c == 0)
    def _():
        qbuf[0:halo, :] = jnp.zeros((halo, ML_WIDTH), F32)
        kbuf[0:halo, :] = jnp.zeros((halo, ML_WIDTH), F32)
        st_scr[...] = jnp.zeros(st_scr.shape, F32)
        m_scr[...] = jnp.zeros(m_scr.shape, F32)

    qbuf[halo:halo + L, :] = q_ref[...].astype(F32)
    kbuf[halo:halo + L, :] = k_ref[...].astype(F32)

    gp = g_ref[...] + bias_ref[...]
    lane = lax.broadcasted_iota(jnp.int32, gp.shape, 1)
    gl = jnp.where(lane < ML_HEADS, gp, _log_sigmoid(gp))
    r_i = lax.broadcasted_iota(jnp.int32, (L, L), 0)
    c_i = lax.broadcasted_iota(jnp.int32, (L, L), 1)
    tril = r_i >= c_i
    csum = jnp.dot(jnp.where(tril, 1.0, 0.0), gl, precision=HIGHEST,
                   preferred_element_type=F32)
    gl_t = gl.T
    csum_t = csum.T
    ones_col = (lax.broadcasted_iota(jnp.int32, (L, LANES), 1) == 0).astype(F32)

    for h in range(ML_HEADS):
        cs = slice(h * dh, (h + 1) * dh)

        def conv(buf, part):
            acc = None
            for j in range(ML_CONV):
                off = halo - (ML_CONV - 1) + j
                term = buf[off:off + L, cs] * cw_ref[j:j + 1, part * ML_WIDTH + h * dh:part * ML_WIDTH + (h + 1) * dh]
                acc = term if acc is None else acc + term
            return acc * jax.nn.sigmoid(acc)

        qh = conv(qbuf, 0)
        kh = conv(kbuf, 1) * (dh ** -0.5)
        vh = v_ref[:, cs]
        v_ext = jnp.concatenate([vh.astype(F32), ones_col], axis=1)

        li_col = gl[:, h:h + 1]
        b_col = csum[:, ML_HEADS + h:ML_HEADS + h + 1]
        li_row = gl_t[h:h + 1, :]
        b_row = csum_t[ML_HEADS + h:ML_HEADS + h + 1, :]
        b_tot = b_row[:, L - 1:L]
        m0 = m_scr[h:h + 1, 0:1]

        a_col = b_tot - b_col + li_col
        m_loc = jnp.max(a_col, axis=0, keepdims=True)
        wa = jnp.exp(a_col - m_loc)

        qb = qh.astype(BF16)
        qk = lax.dot_general(qb, kh.astype(BF16), (((1,), (1,)), ((), ())),
                             preferred_element_type=F32)
        log_d = jnp.where(tril, b_col - b_row + li_row, -jnp.inf)
        inter_log = b_col + m0
        m_t = jnp.maximum(inter_log, jnp.max(log_d, axis=1, keepdims=True))
        s_mat = jnp.exp(log_d - m_t) * qk
        w_inter = jnp.exp(inter_log - m_t)
        state = st_scr[h]
        nd = (jnp.dot(s_mat.astype(BF16), v_ext.astype(BF16), preferred_element_type=F32)
              + w_inter * jnp.dot(qb, state.astype(BF16), preferred_element_type=F32))
        num = nd[:, :dh]
        den = nd[:, dh:dh + 1]
        hval = num / jnp.maximum(jnp.abs(den), jnp.exp(-m_t))
        gate = jax.nn.sigmoid(og_ref[:, cs].astype(F32))
        o_ref[:, cs] = (hval * gate).astype(o_ref.dtype)

        m_new = jnp.maximum(b_tot + m0, m_loc)
        s_old = jnp.exp(b_tot + m0 - m_new)
        s_new = jnp.exp(m_loc - m_new)
        upd = jnp.dot(kh.T.astype(BF16), (wa * v_ext).astype(BF16), preferred_element_type=F32)
        st_scr[h] = s_old * state + s_new * upd
        m_scr[h:h + 1, :] = jnp.broadcast_to(m_new, (1, LANES))

    qbuf[0:halo, :] = qbuf[L:L + halo, :]
    kbuf[0:halo, :] = kbuf[L:L + halo, :]


def mlstm(proj, gates, conv_w, i_bias, f_bias):
    t = proj.shape[0]
    L = ML_CHUNK
    wb = ML_WIDTH
    bias = jnp.zeros((1, LANES), F32).at[0, :ML_HEADS].set(i_bias).at[0, ML_HEADS:2 * ML_HEADS].set(f_bias)
    col = lambda off: pl.BlockSpec((L, wb), lambda c, off=off: (c, off // wb))
    return pl.pallas_call(
        _mlstm_kernel,
        out_shape=jax.ShapeDtypeStruct((t, wb), BF16),
        grid=(t // L,),
        in_specs=[col(COL_ML_Q), col(COL_ML_K), col(COL_ML_V), col(COL_ML_O),
                  pl.BlockSpec((L, LANES), lambda c: (c, 0)),
                  pl.BlockSpec((ML_CONV, 2 * wb), lambda c: (0, 0)),
                  pl.BlockSpec((1, LANES), lambda c: (0, 0))],
        out_specs=pl.BlockSpec((L, wb), lambda c: (c, 0)),
        scratch_shapes=[pltpu.VMEM((L + SUBLANES, wb), F32), pltpu.VMEM((L + SUBLANES, wb), F32),
                        pltpu.VMEM((ML_HEADS, ML_HEAD_DIM, ML_EXT), F32),
                        pltpu.VMEM((SUBLANES, LANES), F32)],
        compiler_params=_params(("arbitrary",)),
        name="mlstm",
    )(proj, proj, proj, proj, gates, conv_w, bias)


def _merge_kernel(ysb_ref, ys5_ref, yml_ref, psb_ref, ps5_ref, pml_ref,
                  gsb_ref, gs5_ref, gml_ref, o_ref):
    def branch(y_ref, p_ref, g_ref):
        return (jax.nn.sigmoid(g_ref[...].astype(F32))
                * jnp.dot(y_ref[...], p_ref[...], preferred_element_type=F32))

    o_ref[...] = (branch(ysb_ref, psb_ref, gsb_ref) + branch(ys5_ref, ps5_ref, gs5_ref)
                  + branch(yml_ref, pml_ref, gml_ref)).astype(o_ref.dtype)


def merge(y_sb, y_s5, y_ml, p_sb, p_s5, p_ml, proj):
    t, k = y_sb.shape
    n = p_sb.shape[1]
    tm, tn = min(1024, t), 1024
    ysp = pl.BlockSpec((tm, k), lambda i, j: (i, 0))
    psp = pl.BlockSpec((k, tn), lambda i, j: (0, j))
    gsp = lambda off: pl.BlockSpec((tm, tn), lambda i, j, off=off: (i, off // tn + j))
    return pl.pallas_call(
        _merge_kernel,
        out_shape=jax.ShapeDtypeStruct((t, n), BF16),
        grid=(t // tm, n // tn),
        in_specs=[ysp, ysp, ysp, psp, psp, psp, gsp(COL_G_SB), gsp(COL_G_S5), gsp(COL_G_ML)],
        out_specs=pl.BlockSpec((tm, tn), lambda i, j: (i, j)),
        compiler_params=_params(("parallel", "parallel")),
        name="merge",
    )(y_sb, y_s5, y_ml, p_sb, p_s5, p_ml, proj, proj, proj)


def _route(logits):
    lane = lax.broadcasted_iota(jnp.int32, logits.shape, 1)
    neg = -jnp.inf
    big = jnp.int32(LANES)
    is_g = jnp.logical_and(lane >= N_EXPERTS, lane < N_EXPERTS + N_GROUPS)
    gl = jnp.where(is_g, logits, neg)
    gmax = jnp.max(gl, axis=1, keepdims=True)
    g_idx = jnp.min(jnp.where(gl == gmax, lane, big), axis=1, keepdims=True) - N_EXPERTS
    g_w = 1.0 / jnp.sum(jnp.where(is_g, jnp.exp(gl - gmax), 0.0), axis=1, keepdims=True)
    in_grp = jnp.logical_and(lane >= g_idx * EXPERTS_PER_GROUP, lane < (g_idx + 1) * EXPERTS_PER_GROUP)
    e1 = jnp.where(in_grp, logits, neg)
    v1 = jnp.max(e1, axis=1, keepdims=True)
    i1 = jnp.min(jnp.where(e1 == v1, lane, big), axis=1, keepdims=True)
    e2 = jnp.where(lane == i1, neg, e1)
    v2 = jnp.max(e2, axis=1, keepdims=True)
    i2 = jnp.min(jnp.where(e2 == v2, lane, big), axis=1, keepdims=True)
    ex = jnp.exp(v2 - v1)
    return i1, i2, g_w / (1.0 + ex), g_w * ex / (1.0 + ex)


META_E1, META_E2, META_W1, META_W2, META_POS1, META_POS2 = range(6)


def _outproj_kernel(m_ref, w_ref, x_ref, g1_ref, ng_ref, sh_ref, sc_ref, wr_ref, br_ref,
                    xo_ref, h_ref, meta_ref, cnt_ref, carry):
    @pl.when(pl.program_id(0) == 0)
    def _():
        carry[...] = jnp.zeros(carry.shape, F32)

    y = jnp.dot(m_ref[...], w_ref[...], preferred_element_type=F32)
    x = x_ref[...] + g1_ref[...] * y
    xo_ref[...] = x
    h = _norm_mod(x, ng_ref[...], sh_ref[...], sc_ref[...])
    h_ref[...] = h
    logits = jnp.dot(h.astype(BF16), wr_ref[...], preferred_element_type=F32) + br_ref[...]
    i1, i2, w1, w2 = _route(logits)

    tm = logits.shape[0]
    lane = lax.broadcasted_iota(jnp.int32, logits.shape, 1)
    picked = jnp.logical_or(lane == i1, lane == i2)
    r_i = lax.broadcasted_iota(jnp.int32, (tm, tm), 0)
    c_i = lax.broadcasted_iota(jnp.int32, (tm, tm), 1)
    earlier = jnp.where(r_i > c_i, 1.0, 0.0).astype(BF16)
    before = jnp.dot(earlier, jnp.where(picked, 1.0, 0.0).astype(BF16), preferred_element_type=F32)
    before = before + carry[0:1, :]
    pos1 = jnp.sum(jnp.where(lane == i1, before, 0.0), axis=1, keepdims=True)
    pos2 = jnp.sum(jnp.where(lane == i2, before, 0.0), axis=1, keepdims=True)
    total = carry[0:1, :] + jnp.sum(jnp.where(picked, 1.0, 0.0), axis=0, keepdims=True)
    carry[...] = jnp.broadcast_to(total, carry.shape)
    cnt_ref[...] = jnp.broadcast_to(total, cnt_ref.shape)

    meta = jnp.zeros(logits.shape, F32)
    for k, val in ((META_E1, i1.astype(F32)), (META_E2, i2.astype(F32)), (META_W1, w1), (META_W2, w2),
                   (META_POS1, pos1), (META_POS2, pos2)):
        meta = jnp.where(lane == k, val, meta)
    meta_ref[...] = meta


def out_proj(merged, w_out, x, gate1, norm_g, shift, scale, w_route, b_route):
    t, d = x.shape
    tm = min(256, t)
    row = pl.BlockSpec((tm, d), lambda i: (i, 0))
    vec = pl.BlockSpec((1, d), lambda i: (0, 0))
    return pl.pallas_call(
        _outproj_kernel,
        out_shape=(jax.ShapeDtypeStruct((t, d), F32), jax.ShapeDtypeStruct((t, d), F32),
                   jax.ShapeDtypeStruct((t, LANES), F32), jax.ShapeDtypeStruct((SUBLANES, LANES), F32)),
        grid=(t // tm,),
        in_specs=[row, pl.BlockSpec((d, d), lambda i: (0, 0)), row, vec, vec, vec, vec,
                  pl.BlockSpec((d, LANES), lambda i: (0, 0)), pl.BlockSpec((1, LANES), lambda i: (0, 0))],
        out_specs=(row, row, pl.BlockSpec((tm, LANES), lambda i: (i, 0)),
                   pl.BlockSpec((SUBLANES, LANES), lambda i: (0, 0))),
        scratch_shapes=[pltpu.VMEM((SUBLANES, LANES), F32)],
        compiler_params=_params(("arbitrary",)),
        name="out_proj",
    )(merged, w_out, x, gate1, norm_g, shift, scale, w_route, b_route)


def moe_plan(meta, counts, n_tiles):
    tile = MOE_TILE
    as_int = lambda k: meta[:, k].astype(jnp.int32)
    cnt = counts[0, :N_EXPERTS].astype(jnp.int32)
    padded = ((cnt + tile - 1) // tile) * tile
    seg_end = jnp.cumsum(padded)
    seg_off = seg_end - padded
    n_used = (seg_end[-1] // tile).reshape(1)
    starts = jnp.arange(n_tiles, dtype=jnp.int32) * tile
    tile_expert = jnp.sum((starts[:, None] >= seg_end[None, :]).astype(jnp.int32), axis=1)
    tile_expert = jnp.minimum(tile_expert, N_EXPERTS - 1)
    picks = (as_int(META_E1), as_int(META_E2), as_int(META_POS1), as_int(META_POS2), seg_off)
    return picks, tile_expert, n_used


def _row_copy(src, src_row, dst, dst_row, sem):
    return pltpu.make_async_copy(src.at[pl.ds(src_row, 1)], dst.at[pl.ds(dst_row, 1)], sem)


def _experts_kernel(e1_ref, e2_ref, p1_ref, p2_ref, off_ref, te_ref, nu_ref,
                    h_ref, w1_ref, w3_ref, w2_ref, y_ref, token_of, buf, sems, w1b, w3b, w2b):
    j = pl.program_id(0)
    tile = MOE_TILE
    n_used = nu_ref[0]

    @pl.when(j == 0)
    def _():
        def clear(s, carry):
            token_of[s] = 0
            return carry

        def fill(t, carry):
            token_of[off_ref[e1_ref[t]] + p1_ref[t]] = t
            token_of[off_ref[e2_ref[t]] + p2_ref[t]] = t
            return carry

        lax.fori_loop(0, token_of.shape[0], clear, 0, unroll=8)
        lax.fori_loop(0, e1_ref.shape[0], fill, 0, unroll=4)

        def first(r, carry):
            _row_copy(h_ref, token_of[r], buf.at[0], r, sems.at[0]).start()
            return carry

        lax.fori_loop(0, tile, first, 0, unroll=4)

    def tile_done(slot):
        return pltpu.make_async_copy(h_ref.at[pl.ds(0, tile)], buf.at[slot], sems.at[slot])

    def run_tile(slot):
        base = jnp.minimum(j + 1, n_used - 1) * tile
        for r in range(tile):
            _row_copy(h_ref, token_of[base + r], buf.at[1 - slot], r, sems.at[1 - slot]).start(priority=r % 2)
        tile_done(slot).wait()
        x = buf[slot].astype(BF16)
        a = jnp.dot(x, w1b[...], preferred_element_type=F32)
        b = jnp.dot(x, w3b[...], preferred_element_type=F32)
        hid = (a * jax.nn.sigmoid(a) * b).astype(BF16)
        y_ref[...] = jnp.dot(hid, w2b[...], preferred_element_type=F32)

        @pl.when(j == n_used - 1)
        def _():
            tile_done(1 - slot).wait()

    @pl.when(j < n_used)
    def _():
        @pl.when(jnp.logical_or(j == 0, te_ref[j] != te_ref[jnp.maximum(j - 1, 0)]))
        def _():
            w1b[...] = w1_ref[...].astype(BF16)
            w3b[...] = w3_ref[...].astype(BF16)
            w2b[...] = w2_ref[...].astype(BF16)

        for slot in range(2):
            pl.when(j % 2 == slot)(functools.partial(run_tile, slot))

    @pl.when(j >= n_used)
    def _():
        y_ref[...] = jnp.zeros(y_ref.shape, y_ref.dtype)


def moe_experts(h, picks, tile_expert, n_used, w1, w3, w2, n_slots):
    t, d = h.shape
    ne, _, ff = w1.shape
    tile = MOE_TILE
    n_pref = len(picks) + 2

    def wspec(r, c):
        def index(j, *pref):
            te, nu = pref[-2], pref[-1]
            return (te[jnp.minimum(j, nu[0] - 1)], 0, 0)
        return pl.BlockSpec((None, r, c), index)

    return pl.pallas_call(
        _experts_kernel,
        out_shape=jax.ShapeDtypeStruct((n_slots, d), F32),
        grid_spec=pltpu.PrefetchScalarGridSpec(
            num_scalar_prefetch=n_pref,
            grid=(n_slots // tile,),
            in_specs=[pl.BlockSpec(memory_space=pl.ANY), wspec(d, ff), wspec(d, ff), wspec(ff, d)],
            out_specs=pl.BlockSpec((tile, d), lambda j, *pref: (j, 0)),
            scratch_shapes=[pltpu.SMEM((n_slots,), jnp.int32), pltpu.VMEM((2, tile, d), F32),
                            pltpu.SemaphoreType.DMA((2,)),
                            pltpu.VMEM((d, ff), BF16), pltpu.VMEM((d, ff), BF16), pltpu.VMEM((ff, d), BF16)]),
        compiler_params=_params(("arbitrary",)),
        name="moe_experts",
    )(*picks, tile_expert, n_used, h, w1, w3, w2)


def _combine_kernel(e1_ref, e2_ref, p1_ref, p2_ref, off_ref, y_ref, x_ref, meta_ref, g2_ref, fg_ref,
                    o_ref, buf, sems, *, final_norm):
    i = pl.program_id(0)
    tm = x_ref.shape[0]

    def gather(tile_idx, slot):
        base = tile_idx * tm

        def issue(r, carry):
            t = base + r
            _row_copy(y_ref, off_ref[e1_ref[t]] + p1_ref[t], buf.at[slot, 0], r, sems.at[slot]).start(priority=0)
            _row_copy(y_ref, off_ref[e2_ref[t]] + p2_ref[t], buf.at[slot, 1], r, sems.at[slot]).start(priority=1)
            return carry

        lax.fori_loop(0, tm, issue, 0, unroll=4)

    @pl.when(i == 0)
    def _():
        gather(0, 0)

    @pl.when(i + 1 < pl.num_programs(0))
    def _():
        gather(i + 1, (i + 1) % 2)

    slot = i % 2
    for k in range(2):
        pltpu.make_async_copy(y_ref.at[pl.ds(0, tm)], buf.at[slot, k], sems.at[slot]).wait()

    meta = meta_ref[...]
    mix = meta[:, META_W1:META_W1 + 1] * buf[slot, 0] + meta[:, META_W2:META_W2 + 1] * buf[slot, 1]
    x = x_ref[...] + g2_ref[...] * mix
    if final_norm:
        ms = jnp.mean(x * x, axis=-1, keepdims=True)
        x = x * lax.rsqrt(ms + EPS) * fg_ref[...]
    o_ref[...] = x


def moe_combine(ys, picks, x, meta, gate2, final_g, final_norm):
    t, d = x.shape
    tm = min(256, t)
    row = pl.BlockSpec((tm, d), lambda i, *pref: (i, 0))
    vec = pl.BlockSpec((1, d), lambda i, *pref: (0, 0))
    return pl.pallas_call(
        functools.partial(_combine_kernel, final_norm=final_norm),
        out_shape=jax.ShapeDtypeStruct((t, d), F32),
        grid_spec=pltpu.PrefetchScalarGridSpec(
            num_scalar_prefetch=len(picks),
            grid=(t // tm,),
            in_specs=[pl.BlockSpec(memory_space=pl.ANY), row,
                      pl.BlockSpec((tm, LANES), lambda i, *pref: (i, 0)), vec, vec],
            out_specs=row,
            scratch_shapes=[pltpu.VMEM((2, 2, tm, d), F32), pltpu.SemaphoreType.DMA((2,))]),
        compiler_params=_params(("arbitrary",)),
        name="moe_combine",
    )(*picks, ys, x, meta, gate2, final_g)


def kernel(x, c, norm_mix_g, norm_moe_g, final_g, w_ada, b_ada, w_in, ml_conv, ml_i_bias, ml_f_bias,
           s5_lam_re, s5_lam_im, s5_log_dt, s5_b_re, s5_b_im, s5_c_re, s5_c_im, s5_d, s5_w_glu,
           p_sb, p_s5, p_ml, w_out, moe_w_group, moe_b_group, moe_w_router, moe_b_router,
           moe_w1, moe_w3, moe_w2):
    bsz, seq, d = x.shape
    assert bsz == 1 and d == D_MODEL
    assert seq % max(ML_CHUNK, SB_BLOCK, S5_CHUNK * SUBLANES) == 0
    depth = w_in.shape[0]
    xt = x.reshape(seq, d)
    mod = ada_mod(c, w_ada, b_ada)
    for l in range(depth):
        shift1, scale1, gate1, shift2, scale2, gate2 = [mod[l, :, i * d:(i + 1) * d] for i in range(6)]
        w_l = w_in[l]
        w_main = jnp.concatenate([w_l[:, :WIN_S5_U], w_l[:, WIN_ML_Q:WIN_GATES], w_l[:, WIN_G_SB:]],
                                 axis=1).astype(BF16)
        w_gate = jnp.pad(w_l[:, WIN_GATES:WIN_G_SB], ((0, 0), (0, LANES - N_GATE_COLS))).astype(BF16)
        w_s5_t = w_l[:, WIN_S5_U:WIN_ML_Q].T.astype(BF16)
        proj, gates, u_t = in_proj(xt, norm_mix_g[l].reshape(1, d), shift1, scale1, w_main, w_gate, w_s5_t)

        y_sb = sb_attention(proj)

        y_t = s5_scan(u_t, *s5_weights(s5_lam_re[l], s5_lam_im[l], s5_log_dt[l], s5_b_re[l], s5_b_im[l],
                                       s5_c_re[l], s5_c_im[l], s5_d[l]))
        y_s5 = s5_glu(y_t, s5_w_glu[l].T.astype(BF16))

        y_ml = mlstm(proj, gates, ml_conv[l], ml_i_bias[l], ml_f_bias[l])

        merged = merge(y_sb, y_s5, y_ml, p_sb[l].astype(BF16), p_s5[l].astype(BF16), p_ml[l].astype(BF16), proj)

        w_route = jnp.pad(jnp.concatenate([moe_w_router[l], moe_w_group[l]], axis=1),
                          ((0, 0), (0, LANES - N_EXPERTS - N_GROUPS))).astype(BF16)
        b_route = jnp.pad(jnp.concatenate([moe_b_router[l], moe_b_group[l]]),
                          (0, LANES - N_EXPERTS - N_GROUPS)).reshape(1, LANES)
        xt, h2, meta, counts = out_proj(merged, w_out[l].astype(BF16), xt, gate1, norm_moe_g[l].reshape(1, d),
                                        shift2, scale2, w_route, b_route)
        n_slots = 2 * seq + N_EXPERTS * MOE_TILE
        picks, tile_expert, n_used = moe_plan(meta, counts, n_slots // MOE_TILE)
        ys = moe_experts(h2, picks, tile_expert, n_used, moe_w1[l], moe_w3[l], moe_w2[l], n_slots)
        xt = moe_combine(ys, picks, xt, meta, gate2, final_g.reshape(1, d), final_norm=(l == depth - 1))
    return xt.reshape(bsz, seq, d)
```
